```python
import jax, jax.numpy as jnp
from jax import lax
import numpy as np

D_MODEL = 1024
BATCH = 1
SEQ = 16384
DEPTH = 1
DEC_BATCH = 128
DEC_SEQ = 8
PAST_LEN = 16384
PAGE_SIZE = 128

N_META = 16
HG_HEADS = 8
HG_DK = 128
HG_DV = D_MODEL // HG_HEADS
HG_WIDTH = HG_HEADS * HG_DK
HG_VWIDTH = HG_HEADS * HG_DV
HG_CHUNK = 64
ATT_HEADS = 16
ATT_KV_HEADS = 4
GROUP = ATT_HEADS // ATT_KV_HEADS
HEAD_DIM = D_MODEL // ATT_HEADS
WINDOW = 128
ROPE_THETA = 10000.0
PEER_HEADS = 8
PEER_NKEYS = 128
PEER_EXPERTS = PEER_NKEYS * PEER_NKEYS
PEER_DKEY = 256
PEER_TOPK = 16
PEER_BLOCK = 128
EPS = 1e-6
COL_SIZES = (HG_WIDTH, HG_WIDTH, HG_VWIDTH, HG_VWIDTH,
             ATT_HEADS * HEAD_DIM, ATT_KV_HEADS * HEAD_DIM, ATT_KV_HEADS * HEAD_DIM,
             D_MODEL, D_MODEL)
PROJ_COLS = sum(COL_SIZES)
COL_SPLITS = tuple(int(c) for c in np.cumsum(COL_SIZES)[:-1])

kernel_name = 'hgrn2_swa_sink_peer_hybrid_step'

F32 = jnp.float32


def _rmsnorm(x, w):
    xf = x.astype(F32)
    y = xf * lax.rsqrt(jnp.mean(xf * xf, axis=-1, keepdims=True) + EPS)
    return (y * w.astype(F32)).astype(x.dtype)


def _rope(x, pos):
    half = x.shape[-1] // 2
    inv = ROPE_THETA ** (-jnp.arange(half, dtype=F32) / half)
    ang = pos.astype(F32)[:, None] * inv[None, :]
    cos = jnp.cos(ang)[None, :, None, :]
    sin = jnp.sin(ang)[None, :, None, :]
    xf = x.astype(F32)
    x1, x2 = xf[..., :half], xf[..., half:]
    return jnp.concatenate([x1 * cos - x2 * sin, x2 * cos + x1 * sin], axis=-1).astype(x.dtype)


def _mixer_projection(h, norm_w, w_in):
    hn = _rmsnorm(h, norm_w)
    proj = jnp.einsum('btd,dc->btc', hn, w_in)
    return jnp.split(proj, COL_SPLITS, axis=-1)


def _hgrn_features(pq, pf, pi, lb):
    B, T, _ = pq.shape
    q = jax.nn.silu(pq.astype(F32)).reshape(B, T, HG_HEADS, HG_DK)
    f = lb + (1.0 - lb) * jax.nn.sigmoid(pf.astype(F32))
    logf = jnp.log(f).reshape(B, T, HG_HEADS, HG_DK)
    k = (1.0 - f).reshape(B, T, HG_HEADS, HG_DK)
    v = pi.astype(F32).reshape(B, T, HG_HEADS, HG_DV)
    return q, logf, k, v


def _hgrn_chunk(S, q, logf, k, v):
    L = q.shape[1]
    b = jnp.cumsum(logf, axis=1)
    o_inter = jnp.einsum('blhk,bhkv->blhv', q * jnp.exp(b), S)
    causal = jnp.tril(jnp.ones((L, L), dtype=bool))
    diff = b[:, :, None] - b[:, None, :]
    decay = jnp.exp(jnp.where(causal[None, :, :, None, None], diff, -jnp.inf))
    scores = jnp.einsum('bthk,btshk,bshk->bhts', q, decay, k)
    o_intra = jnp.einsum('bhts,bshv->bthv', scores, v)
    b_last = b[:, -1]
    S_new = jnp.exp(b_last)[..., None] * S + jnp.einsum(
        'bshk,bshv->bhkv', k * jnp.exp(b_last[:, None] - b), v)
    return S_new, o_inter + o_intra


def _hgrn_prompt(q, logf, k, v):
    B, T = q.shape[:2]
    S0 = jnp.zeros((B, HG_HEADS, HG_DK, HG_DV), F32)
    S, o_meta = _hgrn_chunk(S0, q[:, :N_META], logf[:, :N_META], k[:, :N_META], v[:, :N_META])
    nc = (T - N_META) // HG_CHUNK

    def chunks(a):
        a = a[:, N_META:]
        return a.reshape(B, nc, HG_CHUNK, *a.shape[2:]).swapaxes(0, 1)

    S, o = lax.scan(lambda s, xs: _hgrn_chunk(s, *xs), S,
                    (chunks(q), chunks(logf), chunks(k), chunks(v)))
    o = o.swapaxes(0, 1).reshape(B, nc * HG_CHUNK, HG_HEADS, HG_DV)
    return jnp.concatenate([o_meta, o], axis=1), S


def _hgrn_readout(o, pg, onorm):
    B, T = o.shape[:2]
    on = o * lax.rsqrt(jnp.mean(o * o, axis=-1, keepdims=True) + EPS) * onorm.astype(F32)
    gate = jax.nn.silu(pg.astype(F32)).reshape(B, T, HG_HEADS, HG_DV)
    return (on * gate).reshape(B, T, HG_VWIDTH).astype(pg.dtype)


def _sink_attention(q, k, v, mask, sinks):
    s = jnp.einsum('bnqhgd,bnkhd->bnhgqk', q.astype(F32), k.astype(F32)) * (HEAD_DIM ** -0.5)
    s = jnp.where(mask[None, :, None, None], s, -jnp.inf)
    sink = jnp.broadcast_to(sinks.astype(F32).reshape(1, 1, ATT_KV_HEADS, GROUP, 1, 1),
                            s.shape[:-1] + (1,))
    p = jax.nn.softmax(jnp.concatenate([s, sink], axis=-1), axis=-1)[..., :-1]
    return jnp.einsum('bnhgqk,bnkhd->bnqhgd', p, v.astype(F32))


def _swa_prompt(q, k, v, sinks):
    B, T = q.shape[:2]
    nb = -(-T // WINDOW)
    pad = nb * WINDOW - T
    qb = jnp.pad(q, ((0, 0), (0, pad), (0, 0), (0, 0))).reshape(
        B, nb, WINDOW, ATT_KV_HEADS, GROUP, HEAD_DIM)

    def band(a):
        a = jnp.pad(a, ((0, 0), (WINDOW, pad), (0, 0), (0, 0))).reshape(
            B, nb + 1, WINDOW, ATT_KV_HEADS, HEAD_DIM)
        return jnp.concatenate([a[:, :-1], a[:, 1:]], axis=2)

    blk = jnp.arange(nb)[:, None] * WINDOW
    qpos = blk + jnp.arange(WINDOW)[None, :]
    kpos = blk - WINDOW + jnp.arange(2 * WINDOW)[None, :]
    rel = qpos[:, :, None] - kpos[:, None, :]
    mask = (rel >= 0) & (rel < WINDOW) & (kpos[:, None, :] >= 0)
    o = _sink_attention(qb, band(k), band(v), mask, sinks)
    return o.reshape(B, nb * WINDOW, ATT_HEADS * HEAD_DIM)[:, :T].astype(q.dtype)


def _merge(h, o_a, o_b, g_a, g_b, w_out):
    m = jax.nn.sigmoid(g_a) * o_a + jax.nn.sigmoid(g_b) * o_b
    return h + jnp.einsum('btd,de->bte', m, w_out)


def _peer(x, wq, subkeys, u, v):
    B, T, D = x.shape
    n = B * T
    nblk = -(-n // PEER_BLOCK)
    xb_all = jnp.pad(x.reshape(n, D), ((0, nblk * PEER_BLOCK - n), (0, 0))).reshape(nblk, PEER_BLOCK, D)

    def block(xb):
        qh = jnp.einsum('nd,dc->nc', xb, wq).reshape(PEER_BLOCK, PEER_HEADS, 2, PEER_DKEY // 2)
        s = jnp.einsum('nhpc,hpkc->nhpk', qh.astype(F32), subkeys.astype(F32))
        s1, i1 = lax.top_k(s[:, :, 0], PEER_TOPK)
        s2, i2 = lax.top_k(s[:, :, 1], PEER_TOPK)
        cand = (s1[..., :, None] + s2[..., None, :]).reshape(PEER_BLOCK, PEER_HEADS, PEER_TOPK * PEER_TOPK)
        cid = (i1[..., :, None] * PEER_NKEYS + i2[..., None, :]).reshape(
            PEER_BLOCK, PEER_HEADS, PEER_TOPK * PEER_TOPK)
        sc, j = lax.top_k(cand, PEER_TOPK)
        eid = jnp.take_along_axis(cid, j, axis=-1)
        g = jax.nn.softmax(sc, axis=-1)
        act = jax.nn.gelu(jnp.einsum('nd,nhkd->nhk', xb, u[eid]).astype(F32), approximate=False)
        w = (g * act).astype(xb.dtype)
        return jnp.einsum('nhk,nhkd->nd', w, v[eid])

    y = lax.map(block, xb_all)
    return y.reshape(nblk * PEER_BLOCK, D)[:n].reshape(B, T, D).astype(x.dtype)


def setup_inputs(seed: int = 0) -> dict:
    key = jax.random.key(seed)
    ks = jax.random.split(key, 20)
    win = min(WINDOW, PAST_LEN)
    nrm = lambda k, shape, scale: jax.random.normal(k, shape, F32) * scale
    return {
        'x_prompt': nrm(ks[0], (BATCH, SEQ, D_MODEL), 1.0),
        'x_sample': nrm(ks[1], (DEC_BATCH, DEC_SEQ, D_MODEL), 1.0),
        'state_hgrn': nrm(ks[2], (DEPTH, DEC_BATCH, HG_HEADS, HG_DK, HG_DV), 0.5),
        'cache_swa_k': nrm(ks[3], (DEPTH, DEC_BATCH, win, ATT_KV_HEADS, HEAD_DIM), 1.0),
        'cache_swa_v': nrm(ks[4], (DEPTH, DEC_BATCH, win, ATT_KV_HEADS, HEAD_DIM), 1.0),
        'meta_tokens': nrm(ks[5], (N_META, D_MODEL), 1.0),
        'w_in': nrm(ks[6], (DEPTH, D_MODEL, PROJ_COLS), D_MODEL ** -0.5),
        'hgrn_lb': 1.0 + nrm(ks[7], (DEPTH + 1, HG_WIDTH), 0.1),
        'hgrn_onorm': 1.0 + nrm(ks[8], (DEPTH, HG_DV), 0.1),
        'attn_sinks': nrm(ks[9], (DEPTH, ATT_HEADS), 1.0),
        'w_out': nrm(ks[10], (DEPTH, D_MODEL, D_MODEL), D_MODEL ** -0.5),
        'norm_mix': 1.0 + nrm(ks[11], (DEPTH, D_MODEL), 0.1),
        'norm_ffn': 1.0 + nrm(ks[12], (DEPTH, D_MODEL), 0.1),
        'peer_query': nrm(ks[13], (DEPTH, D_MODEL, PEER_HEADS * PEER_DKEY), D_MODEL ** -0.5),
        'peer_subkeys': nrm(ks[14], (DEPTH, PEER_HEADS, 2, PEER_NKEYS, PEER_DKEY // 2),
                            (PEER_DKEY // 2) ** -0.5),
        'peer_u': nrm(ks[15], (DEPTH, PEER_EXPERTS, D_MODEL), D_MODEL ** -0.5),
        'peer_v': nrm(ks[16], (DEPTH, PEER_EXPERTS, D_MODEL), PEER_HEADS ** -0.5),
        'norm_final': 1.0 + nrm(ks[17], (D_MODEL,), 0.1),
    }


def reference(x_prompt, x_sample, state_hgrn, cache_swa_k, cache_swa_v, meta_tokens, w_in,
              hgrn_lb, hgrn_onorm, attn_sinks, w_out, norm_mix, norm_ffn, peer_query,
              peer_subkeys, peer_u, peer_v, norm_final):
    lb_all = jnp.cumsum(jax.nn.softmax(hgrn_lb.astype(F32), axis=0), axis=0)

    B = x_prompt.shape[0]
    hp = jnp.concatenate([jnp.broadcast_to(meta_tokens.astype(x_prompt.dtype)[None],
                                           (B, N_META, D_MODEL)), x_prompt], axis=1)
    T = hp.shape[1]
    pos_p = jnp.arange(T)
    wp = min(WINDOW, T)
    sp_list, kp_list, vp_list = [], [], []
    for l in range(DEPTH):
        pq, pf, pi, pg, aq, ak, av, ga, gb = _mixer_projection(hp, norm_mix[l], w_in[l])
        hq, hlogf, hk, hv = _hgrn_features(pq, pf, pi, lb_all[l])
        o, s_fin = _hgrn_prompt(hq, hlogf, hk, hv)
        o_a = _hgrn_readout(o, pg, hgrn_onorm[l])
        q = _rope(aq.reshape(B, T, ATT_HEADS, HEAD_DIM), pos_p)
        k = _rope(ak.reshape(B, T, ATT_KV_HEADS, HEAD_DIM), pos_p)
        v = av.reshape(B, T, ATT_KV_HEADS, HEAD_DIM)
        o_b = _swa_prompt(q, k, v, attn_sinks[l])
        hp = _merge(hp, o_a, o_b, ga, gb, w_out[l])
        sp_list.append(s_fin.astype(state_hgrn.dtype))
        kp_list.append(k[:, T - wp:])
        vp_list.append(v[:, T - wp:])
        if l == DEPTH - 1:
            hp = hp[:, N_META:]
        hp = hp + _peer(_rmsnorm(hp, norm_ffn[l]), peer_query[l], peer_subkeys[l], peer_u[l], peer_v[l])
    y_prompt = _rmsnorm(hp, norm_final)

    Bd, L = x_sample.shape[:2]
    hs = x_sample
    pos_s = PAST_LEN + jnp.arange(L)
    wc = cache_swa_k.shape[2]
    kpos = PAST_LEN - wc + jnp.arange(wc + L)
    rel = pos_s[:, None] - kpos[None, :]
    mask_s = ((rel >= 0) & (rel < WINDOW))[None]
    ss_list, ks_list, vs_list = [], [], []
    for l in range(DEPTH):
        pq, pf, pi, pg, aq, ak, av, ga, gb = _mixer_projection(hs, norm_mix[l], w_in[l])
        hq, hlogf, hk, hv = _hgrn_features(pq, pf, pi, lb_all[l])
        s_new, o = _hgrn_chunk(state_hgrn[l].astype(F32), hq, hlogf, hk, hv)
        o_a = _hgrn_readout(o, pg, hgrn_onorm[l])
        q = _rope(aq.reshape(Bd, L, ATT_HEADS, HEAD_DIM), pos_s)
        k = _rope(ak.reshape(Bd, L, ATT_KV_HEADS, HEAD_DIM), pos_s)
        v = av.reshape(Bd, L, ATT_KV_HEADS, HEAD_DIM)
        kc = jnp.concatenate([cache_swa_k[l].astype(k.dtype), k], axis=1)
        vc = jnp.concatenate([cache_swa_v[l].astype(v.dtype), v], axis=1)
        o_b = _sink_attention(q.reshape(Bd, 1, L, ATT_KV_HEADS, GROUP, HEAD_DIM),
                              kc[:, None], vc[:, None], mask_s, attn_sinks[l])
        o_b = o_b.reshape(Bd, L, ATT_HEADS * HEAD_DIM).astype(hs.dtype)
        hs = _merge(hs, o_a, o_b, ga, gb, w_out[l])
        hs = hs + _peer(_rmsnorm(hs, norm_ffn[l]), peer_query[l], peer_subkeys[l], peer_u[l], peer_v[l])
        ss_list.append(s_new.astype(state_hgrn.dtype))
        ks_list.append(kc[:, L:])
        vs_list.append(vc[:, L:])
    y_sample = _rmsnorm(hs, norm_final)

    state_hgrn_prompt = jnp.stack(sp_list)
    state_hgrn_sample = jnp.stack(ss_list)
    cache_swa_k_prompt = jnp.stack(kp_list)
    cache_swa_k_sample = jnp.stack(ks_list)
    cache_swa_v_prompt = jnp.stack(vp_list)
    cache_swa_v_sample = jnp.stack(vs_list)
    return (y_prompt, y_sample, state_hgrn_prompt, state_hgrn_sample,
            cache_swa_k_prompt, cache_swa_k_sample, cache_swa_v_prompt, cache_swa_v_sample)
```

```python
import functools

import jax
import jax.numpy as jnp
import numpy as np
from jax import lax
from jax.experimental import pallas as pl
from jax.experimental.pallas import tpu as pltpu

F32 = jnp.float32
BF16 = jnp.bfloat16

D_MODEL = 1024
N_META = 16
HG_HEADS = 8
HG_DK = 128
HG_DV = 128
ATT_HEADS = 16
ATT_KV_HEADS = 4
GROUP = ATT_HEADS // ATT_KV_HEADS
HEAD_DIM = 64
KV_WIDTH = ATT_KV_HEADS * HEAD_DIM
WINDOW = 128
ROPE_THETA = 10000.0
PAST_LEN = 16384
PEER_HEADS = 8
PEER_NKEYS = 128
PEER_TOPK = 16
EPS = 1e-6

LANES = 128
VMEM_LIMIT = 56 * 1024 * 1024

COL_PQ, COL_PF, COL_PI, COL_PG, COL_AQ, COL_GA, COL_GB = (i * D_MODEL for i in range(7))
COL_AK = 7 * D_MODEL
COL_AV = COL_AK + KV_WIDTH
PROJ_COLS = COL_AV + KV_WIDTH


def _sigmoid(x):
    return jax.nn.sigmoid(x)


def _rms(x, w):
    return x * lax.rsqrt(jnp.mean(x * x, axis=-1, keepdims=True) + EPS) * w


def _params(*sem):
    return pltpu.CompilerParams(dimension_semantics=sem, vmem_limit_bytes=VMEM_LIMIT)


def _proj_body(x_ref, nw_ref, w_ref, o_ref, xn_ref):
    @pl.when(pl.program_id(1) == 0)
    def _():
        xn_ref[...] = _rms(x_ref[...], nw_ref[...]).astype(BF16)

    o_ref[...] = jnp.dot(xn_ref[...], w_ref[...], preferred_element_type=F32)


def _norm_proj(x, nw, w_b, tm, tn):
    rows, n = x.shape[0], w_b.shape[1]
    return pl.pallas_call(
        _proj_body,
        grid=(rows // tm, n // tn),
        in_specs=[pl.BlockSpec((tm, D_MODEL), lambda i, j: (i, 0)),
                  pl.BlockSpec((1, D_MODEL), lambda i, j: (0, 0)),
                  pl.BlockSpec((D_MODEL, tn), lambda i, j: (0, j))],
        out_specs=pl.BlockSpec((tm, tn), lambda i, j: (i, j)),
        out_shape=jax.ShapeDtypeStruct((rows, n), F32),
        scratch_shapes=[pltpu.VMEM((tm, D_MODEL), BF16)],
        compiler_params=_params("parallel", "arbitrary"),
        name="norm_proj",
    )(x, nw, w_b)


def _pad_rows(x, rows):
    if x.shape[0] >= rows:
        return x
    return jnp.concatenate([x, jnp.zeros((rows - x.shape[0], x.shape[1]), x.dtype)], axis=0)


def _hgrn_body(pq_ref, pf_ref, pi_ref, pg_ref, lb_ref, on_ref, s0_ref, o_ref, sout_ref,
               S, b_s, q_s, k_s, v_s, oi_s, *, C):
    c = pl.program_id(1)

    @pl.when(c == 0)
    def _():
        S[...] = s0_ref[...]

    lbr = lb_ref[...]
    lbe = jnp.exp(lbr - jnp.max(lbr, axis=0, keepdims=True))
    lb = lbe[0:1] / jnp.sum(lbe, axis=0, keepdims=True)

    pq = pq_ref[...]
    pg = pg_ref[...]
    q = pq * _sigmoid(pq)
    f = lb + (1.0 - lb) * _sigmoid(pf_ref[...])
    k = 1.0 - f
    v = pi_ref[...]

    row = lax.broadcasted_iota(jnp.int32, (C, HG_HEADS * HG_DK), 0)
    b = jnp.log(f)
    sh = 1
    while sh < C:
        b = b + jnp.where(row >= sh, pltpu.roll(b, sh, axis=0), 0.0)
        sh *= 2
    blast = b[C - 1:C, :]
    qt = q * jnp.exp(b)
    kt = k * jnp.exp(blast - b)
    eblast = jnp.exp(blast)

    b_s[...] = b
    q_s[...] = q
    k_s[...] = k
    v_s[...] = v

    ones_b = jnp.ones((HG_DK, HG_DV), BF16)

    def intra(t, carry):
        bt = b_s[pl.ds(t, 1), :]
        qrow = q_s[pl.ds(t, 1), :]
        w = jnp.exp(jnp.where(row <= t, bt - b_s[...], -jnp.inf)) * (qrow * k_s[...])
        wb = _pad_rows(w, 16).astype(BF16)
        parts = []
        for h in range(HG_HEADS):
            sl = slice(h * HG_DK, (h + 1) * HG_DK)
            sc = jnp.dot(wb[:, sl], ones_b, preferred_element_type=F32)[:C]
            parts.append(jnp.sum(sc * v_s[:, sl], axis=0, keepdims=True))
        oi_s[pl.ds(t, 1), :] = jnp.concatenate(parts, axis=1)
        return carry

    lax.fori_loop(0, C, intra, 0)

    onw = on_ref[...]
    qtb = _pad_rows(qt, 16).astype(BF16)
    ktb = _pad_rows(kt, 16).astype(BF16)
    vb = _pad_rows(v, 16).astype(BF16)
    for h in range(HG_HEADS):
        sl = slice(h * HG_DK, (h + 1) * HG_DK)
        Sh = S[h]
        o_h = jnp.dot(qtb[:, sl], Sh.astype(BF16), preferred_element_type=F32)[:C] + oi_s[:, sl]
        on = o_h * lax.rsqrt(jnp.mean(o_h * o_h, axis=-1, keepdims=True) + EPS) * onw
        g = pg[:, sl]
        o_ref[:, sl] = on * (g * _sigmoid(g))
        decay = jnp.broadcast_to(eblast[:, sl], (HG_DV, HG_DK)).T
        upd = lax.dot_general(ktb[:, sl], vb[:, sl], (((0,), (0,)), ((), ())),
                              preferred_element_type=F32)
        S[h] = decay * Sh + upd

    @pl.when(c == pl.num_programs(1) - 1)
    def _():
        sout_ref[...] = S[...]


def _hgrn(proj, lb, onorm, state, chunk):
    B, T, _ = proj.shape
    width = HG_HEADS * HG_DK
    col = lambda j: pl.BlockSpec((None, chunk, width), lambda b, c, j=j: (b, c, j))
    st = pl.BlockSpec((None, HG_HEADS, HG_DK, HG_DV), lambda b, c: (b, 0, 0, 0))
    return pl.pallas_call(
        functools.partial(_hgrn_body, C=chunk),
        grid=(B, T // chunk),
        in_specs=[col(COL_PQ // width), col(COL_PF // width), col(COL_PI // width), col(COL_PG // width),
                  pl.BlockSpec(lb.shape, lambda b, c: (0, 0)),
                  pl.BlockSpec((1, HG_DV), lambda b, c: (0, 0)),
                  st],
        out_specs=[pl.BlockSpec((None, chunk, width), lambda b, c: (b, c, 0)), st],
        out_shape=[jax.ShapeDtypeStruct((B, T, width), F32),
                   jax.ShapeDtypeStruct(state.shape, F32)],
        scratch_shapes=[pltpu.VMEM((HG_HEADS, HG_DK, HG_DV), F32)]
                       + [pltpu.VMEM((chunk, width), F32)] * 5,
        compiler_params=_params("parallel", "arbitrary"),
        name="hgrn",
    )(proj, proj, proj, proj, lb, onorm, state)


def _rope_tables(pos):
    half = HEAD_DIM // 2
    inv = ROPE_THETA ** (-jnp.arange(half, dtype=F32) / half)
    ang = pos.astype(F32)[:, None] * inv[None, :]
    cos, sin = jnp.cos(ang), jnp.sin(ang)
    cos = jnp.concatenate([cos, cos, cos, cos], axis=1)
    sin = jnp.concatenate([-sin, sin, -sin, sin], axis=1)
    return cos, sin


def _rope(x, cos, sin):
    lane = lax.broadcasted_iota(jnp.int32, (x.shape[0], LANES), 1)
    first = (lane % HEAD_DIM) < (HEAD_DIM // 2)
    outs = []
    for s in range(x.shape[1] // LANES):
        xs = x[:, s * LANES:(s + 1) * LANES]
        partner = jnp.where(first, pltpu.roll(xs, LANES - HEAD_DIM // 2, axis=1),
                            pltpu.roll(xs, HEAD_DIM // 2, axis=1))
        outs.append(xs * cos + partner * sin)
    return jnp.concatenate(outs, axis=1)


def _sink_softmax_pv(s, sink, vg):
    m = jnp.maximum(jnp.max(s, axis=-1, keepdims=True), sink)
    p = jnp.exp(s - m)
    den = jnp.sum(p, axis=-1, keepdims=True) + jnp.exp(sink - m)
    return jnp.dot(p.astype(BF16), vg, preferred_element_type=F32) / den


def _swa_prompt_body(q_ref, k_ref, v_ref, cos_ref, sin_ref, sink_ref, o_ref, krot_ref, kprev, vprev,
                     *, first_valid):
    i = pl.program_id(0)

    @pl.when(i == 0)
    def _():
        kprev[...] = jnp.zeros_like(kprev)
        vprev[...] = jnp.zeros_like(vprev)

    cos, sin = cos_ref[...], sin_ref[...]
    qr = _rope(q_ref[...], cos, sin)
    kr = _rope(k_ref[...], cos, sin)
    v = v_ref[...]
    krot_ref[...] = kr
    kk = jnp.concatenate([kprev[...], kr], axis=0).astype(BF16)
    vv = jnp.concatenate([vprev[...], v], axis=0).astype(BF16)
    r = lax.broadcasted_iota(jnp.int32, (WINDOW, 2 * WINDOW), 0)
    j = lax.broadcasted_iota(jnp.int32, (WINDOW, 2 * WINDOW), 1)
    mask = (j > r) & (j <= r + WINDOW) & ((i - 1) * WINDOW + j >= first_valid)
    for g in range(ATT_KV_HEADS):
        kg = kk[:, g * HEAD_DIM:(g + 1) * HEAD_DIM]
        vg = vv[:, g * HEAD_DIM:(g + 1) * HEAD_DIM]
        for hh in range(GROUP):
            h = g * GROUP + hh
            qh = qr[:, h * HEAD_DIM:(h + 1) * HEAD_DIM].astype(BF16)
            s = lax.dot_general(qh, kg, (((1,), (1,)), ((), ())),
                                preferred_element_type=F32) * (HEAD_DIM ** -0.5)
            s = jnp.where(mask, s, -jnp.inf)
            o_ref[:, h * HEAD_DIM:(h + 1) * HEAD_DIM] = _sink_softmax_pv(s, sink_ref[h], vg)
    kprev[...] = kr
    vprev[...] = v


def _swa_prompt(proj, cos, sin, sinks, first_valid):
    T = proj.shape[0]
    nb = T // WINDOW
    return pl.pallas_call(
        functools.partial(_swa_prompt_body, first_valid=first_valid),
        grid=(nb,),
        in_specs=[pl.BlockSpec((WINDOW, D_MODEL), lambda i: (i, COL_AQ // D_MODEL)),
                  pl.BlockSpec((WINDOW, KV_WIDTH), lambda i: (i, COL_AK // KV_WIDTH)),
                  pl.BlockSpec((WINDOW, KV_WIDTH), lambda i: (i, COL_AV // KV_WIDTH)),
                  pl.BlockSpec((WINDOW, LANES), lambda i: (i, 0)),
                  pl.BlockSpec((WINDOW, LANES), lambda i: (i, 0)),
                  pl.BlockSpec(memory_space=pltpu.SMEM)],
        out_specs=[pl.BlockSpec((WINDOW, D_MODEL), lambda i: (i, 0)),
                   pl.BlockSpec((WINDOW, KV_WIDTH), lambda i: (i, 0))],
        out_shape=[jax.ShapeDtypeStruct((T, D_MODEL), F32),
                   jax.ShapeDtypeStruct((T, KV_WIDTH), F32)],
        scratch_shapes=[pltpu.VMEM((WINDOW, KV_WIDTH), F32)] * 2,
        compiler_params=_params("arbitrary"),
        name="swa_prompt",
    )(proj, proj, proj, cos, sin, sinks)


def _swa_sample_body(q_ref, k_ref, v_ref, ck_ref, cv_ref, cos_ref, sin_ref, sink_ref, o_ref, krot_ref,
                     *, SB, L, WC):
    cos, sin = cos_ref[...], sin_ref[...]
    qr = _rope(q_ref[...], cos, sin)
    kr = _rope(k_ref[...], cos, sin)
    v = v_ref[...]
    krot_ref[...] = kr
    R = GROUP * L
    rr = lax.broadcasted_iota(jnp.int32, (R, WC + L), 0)
    j = lax.broadcasted_iota(jnp.int32, (R, WC + L), 1)
    rel = rr % L + WC - j
    mask = (rel >= 0) & (rel < WINDOW)
    hrow = lax.broadcasted_iota(jnp.int32, (R, 1), 0) // L
    for b in range(SB):
        rows = slice(b * L, (b + 1) * L)
        for g in range(ATT_KV_HEADS):
            cols = slice(g * HEAD_DIM, (g + 1) * HEAD_DIM)
            keys = jnp.concatenate([ck_ref[b][:, cols], kr[rows, cols]], axis=0).astype(BF16)
            vals = jnp.concatenate([cv_ref[b][:, cols], v[rows, cols]], axis=0).astype(BF16)
            qg = jnp.concatenate(
                [qr[rows, (g * GROUP + hh) * HEAD_DIM:(g * GROUP + hh + 1) * HEAD_DIM]
                 for hh in range(GROUP)], axis=0).astype(BF16)
            sink = jnp.zeros((R, 1), F32)
            for hh in range(GROUP):
                sink = jnp.where(hrow == hh, sink_ref[g * GROUP + hh], sink)
            s = lax.dot_general(qg, keys, (((1,), (1,)), ((), ())),
                                preferred_element_type=F32) * (HEAD_DIM ** -0.5)
            s = jnp.where(mask, s, -jnp.inf)
            o = _sink_softmax_pv(s, sink, vals)
            for hh in range(GROUP):
                h = g * GROUP + hh
                o_ref[rows, h * HEAD_DIM:(h + 1) * HEAD_DIM] = o[hh * L:(hh + 1) * L]


def _swa_sample(proj, cache_k, cache_v, cos, sin, sinks, L, SB):
    rows = proj.shape[0]
    Bd, WC, _ = cache_k.shape
    blk = SB * L
    return pl.pallas_call(
        functools.partial(_swa_sample_body, SB=SB, L=L, WC=WC),
        grid=(Bd // SB,),
        in_specs=[pl.BlockSpec((blk, D_MODEL), lambda i: (i, COL_AQ // D_MODEL)),
                  pl.BlockSpec((blk, KV_WIDTH), lambda i: (i, COL_AK // KV_WIDTH)),
                  pl.BlockSpec((blk, KV_WIDTH), lambda i: (i, COL_AV // KV_WIDTH)),
                  pl.BlockSpec((SB, WC, KV_WIDTH), lambda i: (i, 0, 0)),
                  pl.BlockSpec((SB, WC, KV_WIDTH), lambda i: (i, 0, 0)),
                  pl.BlockSpec((blk, LANES), lambda i: (0, 0)),
                  pl.BlockSpec((blk, LANES), lambda i: (0, 0)),
                  pl.BlockSpec(memory_space=pltpu.SMEM)],
        out_specs=[pl.BlockSpec((blk, D_MODEL), lambda i: (i, 0)),
                   pl.BlockSpec((blk, KV_WIDTH), lambda i: (i, 0))],
        out_shape=[jax.ShapeDtypeStruct((rows, D_MODEL), F32),
                   jax.ShapeDtypeStruct((rows, KV_WIDTH), F32)],
        compiler_params=_params("parallel"),
        name="swa_sample",
    )(proj, proj, proj, cache_k, cache_v, cos, sin, sinks)


def _merge_body(h_ref, oa_ref, ob_ref, ga_ref, gb_ref, w_ref, o_ref):
    m = _sigmoid(ga_ref[...]) * oa_ref[...] + _sigmoid(gb_ref[...]) * ob_ref[...]
    o_ref[...] = h_ref[...] + jnp.dot(m.astype(BF16), w_ref[...], preferred_element_type=F32)


def _merge(h, o_a, o_b, proj, w_out_b, tm):
    rows = h.shape[0]
    row = lambda j: pl.BlockSpec((tm, D_MODEL), lambda i, j=j: (i, j))
    return pl.pallas_call(
        _merge_body,
        grid=(rows // tm,),
        in_specs=[row(0), row(0), row(0), row(COL_GA // D_MODEL), row(COL_GB // D_MODEL),
                  pl.BlockSpec((D_MODEL, D_MODEL), lambda i: (0, 0))],
        out_specs=row(0),
        out_shape=jax.ShapeDtypeStruct((rows, D_MODEL), F32),
        compiler_params=_params("parallel"),
        name="merge",
    )(h, o_a, o_b, proj, proj, w_out_b)


def _top_values(x, n):
    vals = []
    for _ in range(n):
        m = jnp.max(x, axis=0, keepdims=True)
        vals.append(m)
        x = jnp.where(x == m, -jnp.inf, x)
    return jnp.concatenate(vals, axis=0)


def _staircase_sums(t1, t2):
    K = PEER_TOPK
    sub = lax.broadcasted_iota(jnp.int32, (8, t1.shape[1]), 0)
    pieces = [t1[0:1, :] + t2, t1[1:2, :] + t2[0:8, :]]
    for i in range(2, 8):
        pieces.append(jnp.where(sub < K // (i + 1), t1[i:i + 1, :] + t2[0:8, :], -jnp.inf))
    pieces.append(t1[8:16, :] + t2[0:1, :])
    return jnp.concatenate(pieces, axis=0)


def _peer_body(h_ref, nf_ref, nfin_ref, wq_ref, sub_ref, u_ref, vt_ref, o_ref,
               xnT, s1T, s2T, e1T, e2T, tau, G, yT, *, TB, EC):
    e = pl.program_id(1)
    K = PEER_TOPK

    @pl.when(e == 0)
    def _():
        xt = _rms(h_ref[...], nf_ref[...]).T.astype(BF16)
        xnT[...] = xt
        for h in range(PEER_HEADS):
            tops = []
            scores = []
            for p in range(2):
                hp = 2 * h + p
                qhp = jnp.dot(wq_ref[hp * LANES:(hp + 1) * LANES, :], xt, preferred_element_type=F32)
                s = jnp.dot(sub_ref[hp], qhp.astype(BF16), preferred_element_type=F32)
                scores.append(s)
                tops.append(_top_values(s, K))
            t1, t2 = tops
            cand = _staircase_sums(t1, t2)
            c = _top_values(cand, K)
            z = jnp.sum(jnp.exp(c - c[0:1, :]), axis=0, keepdims=True)
            s1T[h] = scores[0]
            s2T[h] = scores[1]
            e1T[h] = jnp.exp(scores[0] - t1[0:1, :]) / z
            e2T[h] = jnp.exp(scores[1] - t2[0:1, :])
            tau[h:h + 1, :] = c[K - 1:K, :]

    act = jnp.dot(u_ref[...], xnT[...], preferred_element_type=F32)
    for al in range(EC // PEER_NKEYS):
        a = e * (EC // PEER_NKEYS) + al
        w = jnp.zeros((PEER_NKEYS, TB), F32)
        for h in range(PEER_HEADS):
            sel = (s1T[h, pl.ds(a, 1), :] + s2T[h]) >= tau[h:h + 1, :]
            w = w + jnp.where(sel, e1T[h, pl.ds(a, 1), :] * e2T[h], 0.0)
        x = act[al * PEER_NKEYS:(al + 1) * PEER_NKEYS, :]
        gelu = 0.5 * x * (1.0 + lax.erf(x * (2.0 ** -0.5)))
        G[al * PEER_NKEYS:(al + 1) * PEER_NKEYS, :] = (w * gelu).astype(BF16)
    contrib = jnp.dot(vt_ref[...], G[...], preferred_element_type=F32)

    @pl.when(e == 0)
    def _():
        yT[...] = contrib

    @pl.when(e > 0)
    def _():
        yT[...] += contrib

    @pl.when(e == pl.num_programs(1) - 1)
    def _():
        h3 = h_ref[...] + yT[...].T
        o_ref[...] = _rms(h3, nfin_ref[...])


def _peer(h2, row0, rows, nf, nfin, wqT_b, sub_b, u_b, vT_b, TB, EC):
    nexp = u_b.shape[0]
    off = row0 // TB
    f = pl.pallas_call(
        functools.partial(_peer_body, TB=TB, EC=EC),
        grid=(rows // TB, nexp // EC),
        in_specs=[pl.BlockSpec((TB, D_MODEL), lambda i, e: (i + off, 0)),
                  pl.BlockSpec((1, D_MODEL), lambda i, e: (0, 0)),
                  pl.BlockSpec((1, D_MODEL), lambda i, e: (0, 0)),
                  pl.BlockSpec(wqT_b.shape, lambda i, e: (0, 0)),
                  pl.BlockSpec(sub_b.shape, lambda i, e: (0, 0, 0)),
                  pl.BlockSpec((EC, D_MODEL), lambda i, e: (e, 0)),
                  pl.BlockSpec((D_MODEL, EC), lambda i, e: (0, e))],
        out_specs=pl.BlockSpec((TB, D_MODEL), lambda i, e: (i, 0)),
        out_shape=jax.ShapeDtypeStruct((rows, D_MODEL), F32),
        scratch_shapes=[pltpu.VMEM((D_MODEL, TB), BF16)]
                       + [pltpu.VMEM((PEER_HEADS, PEER_NKEYS, TB), F32)] * 4
                       + [pltpu.VMEM((PEER_HEADS, TB), F32),
                          pltpu.VMEM((EC, TB), BF16),
                          pltpu.VMEM((D_MODEL, TB), F32)],
        compiler_params=_params("parallel", "arbitrary"),
        name="peer",
    )
    return f(h2, nf, nfin, wqT_b, sub_b, u_b, vT_b)


PEER_TB = 256
PEER_EC = 1024
PROJ_TN = 1280
HGRN_CHUNK = 32
SAMPLE_SB = 8


def _row_tile(rows, cap):
    best = 8
    for t in range(8, cap + 1, 8):
        if rows % t == 0:
            best = t
    return best


def kernel(x_prompt, x_sample, state_hgrn, cache_swa_k, cache_swa_v, meta_tokens, w_in, hgrn_lb,
           hgrn_onorm, attn_sinks, w_out, norm_mix, norm_ffn, peer_query, peer_subkeys, peer_u, peer_v,
           norm_final):
    assert w_in.shape[0] == 1, "single-layer trunk"
    B, SEQ, _ = x_prompt.shape
    Bd, L, _ = x_sample.shape
    assert B == 1
    past = PAST_LEN
    wc = cache_swa_k.shape[2]
    START = PEER_TB
    T = START + SEQ
    first_valid = START - N_META

    sizes = np.cumsum([0, 1024, 1024, 1024, 1024, 1024, 256, 256, 1024, 1024])
    seg = lambda i: w_in[0][:, sizes[i]:sizes[i + 1]]
    w_in_b = jnp.concatenate([seg(0), seg(1), seg(2), seg(3), seg(4), seg(7), seg(8), seg(5), seg(6)],
                             axis=1).astype(BF16)
    w_out_b = w_out[0].astype(BF16)
    wqT_b = peer_query[0].T.astype(BF16)
    sub_b = peer_subkeys[0].reshape(PEER_HEADS * 2, PEER_NKEYS, -1).astype(BF16)
    u_b = peer_u[0].astype(BF16)
    vT_b = peer_v[0].T.astype(BF16)
    nmix = norm_mix[0].reshape(1, D_MODEL)
    nffn = norm_ffn[0].reshape(1, D_MODEL)
    nfin = norm_final.reshape(1, D_MODEL)
    onorm = hgrn_onorm[0].reshape(1, HG_DV)
    sinks = attn_sinks[0]

    hp = jnp.concatenate([jnp.zeros((first_valid, D_MODEL), F32), meta_tokens.astype(F32), x_prompt[0]],
                         axis=0)
    proj_p = _norm_proj(hp, nmix, w_in_b, _row_tile(T, 1280), PROJ_TN)
    oa_p, st_p = _hgrn(proj_p[None], hgrn_lb, onorm,
                       jnp.zeros((1, HG_HEADS, HG_DK, HG_DV), F32), HGRN_CHUNK)
    cos_p, sin_p = _rope_tables(jnp.arange(T) - first_valid)
    ob_p, krot_p = _swa_prompt(proj_p, cos_p, sin_p, sinks, first_valid)
    h2_p = _merge(hp, oa_p[0], ob_p, proj_p, w_out_b, _row_tile(T, 512))
    y_prompt = _peer(h2_p, START, SEQ, nffn, nfin, wqT_b, sub_b, u_b, vT_b, PEER_TB, PEER_EC)[None]
    wp = min(WINDOW, SEQ + N_META)
    kp = krot_p[T - wp:].reshape(1, 1, wp, ATT_KV_HEADS, HEAD_DIM)
    vp = proj_p[T - wp:, COL_AV:COL_AV + KV_WIDTH].reshape(1, 1, wp, ATT_KV_HEADS, HEAD_DIM)

    hs = x_sample.reshape(Bd * L, D_MODEL)
    proj_s = _norm_proj(hs, nmix, w_in_b, _row_tile(Bd * L, 512), PROJ_TN)
    oa_s, st_s = _hgrn(proj_s.reshape(Bd, L, PROJ_COLS), hgrn_lb, onorm, state_hgrn[0], L)
    cos_s, sin_s = _rope_tables(past + jnp.arange(L))
    cos_s, sin_s = jnp.tile(cos_s, (SAMPLE_SB, 1)), jnp.tile(sin_s, (SAMPLE_SB, 1))
    ck = cache_swa_k[0].reshape(Bd, wc, KV_WIDTH)
    cv = cache_swa_v[0].reshape(Bd, wc, KV_WIDTH)
    ob_s, krot_s = _swa_sample(proj_s, ck, cv, cos_s, sin_s, sinks, L, SAMPLE_SB)
    h2_s = _merge(hs, oa_s.reshape(Bd * L, D_MODEL), ob_s, proj_s, w_out_b, _row_tile(Bd * L, 512))
    y_sample = _peer(h2_s, 0, Bd * L, nffn, nfin, wqT_b, sub_b, u_b, vT_b, PEER_TB, PEER_EC)
    y_sample = y_sample.reshape(Bd, L, D_MODEL)
    knew = krot_s.reshape(Bd, L, KV_WIDTH)
    vnew = proj_s[:, COL_AV:COL_AV + KV_WIDTH].reshape(Bd, L, KV_WIDTH)
    ks = jnp.concatenate([ck, knew], axis=1)[:, L:].reshape(1, Bd, wc, ATT_KV_HEADS, HEAD_DIM)
    vs = jnp.concatenate([cv, vnew], axis=1)[:, L:].reshape(1, Bd, wc, ATT_KV_HEADS, HEAD_DIM)

    return (y_prompt, y_sample, st_p[None], st_s[None], kp, ks, vp, vs)
```

```python
import functools

import jax
import jax.numpy as jnp
import numpy as np
from jax import lax
from jax.experimental import pallas as pl
from jax.experimental.pallas import tpu as pltpu

F32 = jnp.float32
BF16 = jnp.bfloat16

D_MODEL = 1024
N_META = 16
HG_HEADS = 8
HG_DK = 128
HG_DV = 128
ATT_HEADS = 16
ATT_KV_HEADS = 4
GROUP = ATT_HEADS // ATT_KV_HEADS
HEAD_DIM = 64
KV_WIDTH = ATT_KV_HEADS * HEAD_DIM
WINDOW = 128
ROPE_THETA = 10000.0
PAST_LEN = 16384
PEER_HEADS = 8
PEER_NKEYS = 128
PEER_TOPK = 16
EPS = 1e-6

LANES = 128
VMEM_LIMIT = 56 * 1024 * 1024

COL_PQ, COL_PF, COL_PI, COL_PG, COL_AQ, COL_GA, COL_GB = (i * D_MODEL for i in range(7))
COL_AK = 7 * D_MODEL
COL_AV = COL_AK + KV_WIDTH
PROJ_COLS = COL_AV + KV_WIDTH


def _sigmoid(x):
    return jax.nn.sigmoid(x)


def _rms(x, w):
    return x * lax.rsqrt(jnp.mean(x * x, axis=-1, keepdims=True) + EPS) * w


def _params(*sem):
    return pltpu.CompilerParams(dimension_semantics=sem, vmem_limit_bytes=VMEM_LIMIT)


def _proj_body(x_ref, nw_ref, w_ref, o_ref, xn_ref):
    @pl.when(pl.program_id(1) == 0)
    def _():
        xn_ref[...] = _rms(x_ref[...], nw_ref[...]).astype(BF16)

    o_ref[...] = jnp.dot(xn_ref[...], w_ref[...], preferred_element_type=F32)


def _norm_proj(x, nw, w_b, tm, tn):
    rows, n = x.shape[0], w_b.shape[1]
    return pl.pallas_call(
        _proj_body,
        grid=(rows // tm, n // tn),
        in_specs=[pl.BlockSpec((tm, D_MODEL), lambda i, j: (i, 0)),
                  pl.BlockSpec((1, D_MODEL), lambda i, j: (0, 0)),
                  pl.BlockSpec((D_MODEL, tn), lambda i, j: (0, j))],
        out_specs=pl.BlockSpec((tm, tn), lambda i, j: (i, j)),
        out_shape=jax.ShapeDtypeStruct((rows, n), F32),
        scratch_shapes=[pltpu.VMEM((tm, D_MODEL), BF16)],
        compiler_params=_params("parallel", "arbitrary"),
        name="norm_proj",
    )(x, nw, w_b)


def _pad_rows(x, rows):
    if x.shape[0] >= rows:
        return x
    return jnp.concatenate([x, jnp.zeros((rows - x.shape[0], x.shape[1]), x.dtype)], axis=0)


def _hgrn_body(pq_ref, pf_ref, pi_ref, pg_ref, lb_ref, on_ref, s0_ref, o_ref, sout_ref, ST, *, C):
    c = pl.program_id(1)
    W2 = 2 * HG_DK
    NG = C // 8

    @pl.when(c == 0)
    def _():
        for h in range(HG_HEADS):
            ST[h] = s0_ref[h].T

    lbr = lb_ref[...]
    lbe = jnp.exp(lbr - jnp.max(lbr, axis=0, keepdims=True))
    lb = lbe[0:1] / jnp.sum(lbe, axis=0, keepdims=True)

    pq = pq_ref[...]
    pg = pg_ref[...]
    q = pq * _sigmoid(pq)
    f = lb + (1.0 - lb) * _sigmoid(pf_ref[...])
    k = 1.0 - f
    v = pi_ref[...]

    row = lax.broadcasted_iota(jnp.int32, (C, HG_HEADS * HG_DK), 0)
    b = jnp.log(f)
    sh = 1
    while sh < C:
        b = b + jnp.where(row >= sh, pltpu.roll(b, sh, axis=0), 0.0)
        sh *= 2
    blast = b[C - 1:C, :]
    qt = q * jnp.exp(b)
    kt = k * jnp.exp(blast - b)
    eblast = jnp.exp(blast)

    r2 = lax.broadcasted_iota(jnp.int32, (W2, W2), 0) // HG_DK
    c2 = lax.broadcasted_iota(jnp.int32, (W2, W2), 1) // HG_DK
    ones_blk = (r2 == c2).astype(BF16)
    acc = [[jnp.zeros((8, W2), F32) for _ in range(NG)] for _ in range(HG_HEADS // 2)]
    for sg in range(NG):
        t0 = 8 * sg
        n = C - t0
        bt, qq = b[t0:, :], q[t0:, :]
        trow = lax.broadcasted_iota(jnp.int32, (n, HG_HEADS * HG_DK), 0) + t0
        pieces = []
        for s in range(t0, t0 + 8):
            e = jnp.exp(jnp.where(trow >= s, bt - b[s:s + 1, :], -jnp.inf))
            pieces.append(e * (qq * k[s:s + 1, :]))
        E = jnp.concatenate(pieces, axis=0).astype(BF16)
        for hp in range(HG_HEADS // 2):
            lanes = slice(hp * W2, (hp + 1) * W2)
            R = jnp.dot(E[:, lanes], ones_blk, preferred_element_type=F32)
            for si in range(8):
                vrow = v[t0 + si:t0 + si + 1, lanes]
                for tg in range(n // 8):
                    blk = R[si * n + 8 * tg:si * n + 8 * tg + 8, :]
                    acc[hp][sg + tg] = acc[hp][sg + tg] + blk * vrow
    o_intra = jnp.concatenate([jnp.concatenate(a, axis=0) for a in acc], axis=1)

    onw = on_ref[...]
    qtb = _pad_rows(qt, 16).astype(BF16)
    ktb = _pad_rows(kt, 16).astype(BF16)
    vb = _pad_rows(v, 16).astype(BF16)
    for h in range(HG_HEADS):
        sl = slice(h * HG_DK, (h + 1) * HG_DK)
        Sh = ST[h]
        o_h = lax.dot_general(qtb[:, sl], Sh.astype(BF16), (((1,), (1,)), ((), ())),
                              preferred_element_type=F32)[:C] + o_intra[:, sl]
        on = o_h * lax.rsqrt(jnp.mean(o_h * o_h, axis=-1, keepdims=True) + EPS) * onw
        g = pg[:, sl]
        o_ref[:, sl] = on * (g * _sigmoid(g))
        upd = lax.dot_general(vb[:, sl], ktb[:, sl], (((0,), (0,)), ((), ())),
                              preferred_element_type=F32)
        ST[h] = Sh * eblast[:, sl] + upd

    @pl.when(c == pl.num_programs(1) - 1)
    def _():
        for h in range(HG_HEADS):
            sout_ref[h] = ST[h].T


def _hgrn(proj, lb, onorm, state, chunk):
    B, T, _ = proj.shape
    width = HG_HEADS * HG_DK
    col = lambda j: pl.BlockSpec((None, chunk, width), lambda b, c, j=j: (b, c, j))
    st = pl.BlockSpec((None, HG_HEADS, HG_DK, HG_DV), lambda b, c: (b, 0, 0, 0))
    return pl.pallas_call(
        functools.partial(_hgrn_body, C=chunk),
        grid=(B, T // chunk),
        in_specs=[col(COL_PQ // width), col(COL_PF // width), col(COL_PI // width), col(COL_PG // width),
                  pl.BlockSpec(lb.shape, lambda b, c: (0, 0)),
                  pl.BlockSpec((1, HG_DV), lambda b, c: (0, 0)),
                  st],
        out_specs=[pl.BlockSpec((None, chunk, width), lambda b, c: (b, c, 0)), st],
        out_shape=[jax.ShapeDtypeStruct((B, T, width), F32),
                   jax.ShapeDtypeStruct(state.shape, F32)],
        scratch_shapes=[pltpu.VMEM((HG_HEADS, HG_DV, HG_DK), F32)],
        compiler_params=_params("parallel", "arbitrary"),
        name="hgrn",
    )(proj, proj, proj, proj, lb, onorm, state)


def _rope_tables(pos):
    half = HEAD_DIM // 2
    inv = ROPE_THETA ** (-jnp.arange(half, dtype=F32) / half)
    ang = pos.astype(F32)[:, None] * inv[None, :]
    cos, sin = jnp.cos(ang), jnp.sin(ang)
    cos = jnp.concatenate([cos, cos, cos, cos], axis=1)
    sin = jnp.concatenate([-sin, sin, -sin, sin], axis=1)
    return cos, sin


def _rope(x, cos, sin):
    lane = lax.broadcasted_iota(jnp.int32, (x.shape[0], LANES), 1)
    first = (lane % HEAD_DIM) < (HEAD_DIM // 2)
    outs = []
    for s in range(x.shape[1] // LANES):
        xs = x[:, s * LANES:(s + 1) * LANES]
        partner = jnp.where(first, pltpu.roll(xs, LANES - HEAD_DIM // 2, axis=1),
                            pltpu.roll(xs, HEAD_DIM // 2, axis=1))
        outs.append(xs * cos + partner * sin)
    return jnp.concatenate(outs, axis=1)


def _sink_softmax_pv(s, sink, vg):
    m = jnp.maximum(jnp.max(s, axis=-1, keepdims=True), sink)
    p = jnp.exp(s - m)
    den = jnp.sum(p, axis=-1, keepdims=True) + jnp.exp(sink - m)
    return jnp.dot(p.astype(BF16), vg, preferred_element_type=F32) / den


def _swa_prompt_body(q_ref, k_ref, v_ref, cos_ref, sin_ref, sink_ref, o_ref, krot_ref, kprev, vprev,
                     *, first_valid):
    i = pl.program_id(0)

    @pl.when(i == 0)
    def _():
        kprev[...] = jnp.zeros_like(kprev)
        vprev[...] = jnp.zeros_like(vprev)

    cos, sin = cos_ref[...], sin_ref[...]
    qr = _rope(q_ref[...], cos, sin)
    kr = _rope(k_ref[...], cos, sin)
    v = v_ref[...]
    krot_ref[...] = kr
    kk = jnp.concatenate([kprev[...], kr], axis=0).astype(BF16)
    vv = jnp.concatenate([vprev[...], v], axis=0).astype(BF16)
    r = lax.broadcasted_iota(jnp.int32, (WINDOW, 2 * WINDOW), 0)
    j = lax.broadcasted_iota(jnp.int32, (WINDOW, 2 * WINDOW), 1)
    mask = (j > r) & (j <= r + WINDOW) & ((i - 1) * WINDOW + j >= first_valid)
    for g in range(ATT_KV_HEADS):
        kg = kk[:, g * HEAD_DIM:(g + 1) * HEAD_DIM]
        vg = vv[:, g * HEAD_DIM:(g + 1) * HEAD_DIM]
        for hh in range(GROUP):
            h = g * GROUP + hh
            qh = qr[:, h * HEAD_DIM:(h + 1) * HEAD_DIM].astype(BF16)
            s = lax.dot_general(qh, kg, (((1,), (1,)), ((), ())),
                                preferred_element_type=F32) * (HEAD_DIM ** -0.5)
            s = jnp.where(mask, s, -jnp.inf)
            o_ref[:, h * HEAD_DIM:(h + 1) * HEAD_DIM] = _sink_softmax_pv(s, sink_ref[h], vg)
    kprev[...] = kr
    vprev[...] = v


def _swa_prompt(proj, cos, sin, sinks, first_valid):
    T = proj.shape[0]
    nb = T // WINDOW
    return pl.pallas_call(
        functools.partial(_swa_prompt_body, first_valid=first_valid),
        grid=(nb,),
        in_specs=[pl.BlockSpec((WINDOW, D_MODEL), lambda i: (i, COL_AQ // D_MODEL)),
                  pl.BlockSpec((WINDOW, KV_WIDTH), lambda i: (i, COL_AK // KV_WIDTH)),
                  pl.BlockSpec((WINDOW, KV_WIDTH), lambda i: (i, COL_AV // KV_WIDTH)),
                  pl.BlockSpec((WINDOW, LANES), lambda i: (i, 0)),
                  pl.BlockSpec((WINDOW, LANES), lambda i: (i, 0)),
                  pl.BlockSpec(memory_space=pltpu.SMEM)],
        out_specs=[pl.BlockSpec((WINDOW, D_MODEL), lambda i: (i, 0)),
                   pl.BlockSpec((WINDOW, KV_WIDTH), lambda i: (i, 0))],
        out_shape=[jax.ShapeDtypeStruct((T, D_MODEL), F32),
                   jax.ShapeDtypeStruct((T, KV_WIDTH), F32)],
        scratch_shapes=[pltpu.VMEM((WINDOW, KV_WIDTH), F32)] * 2,
        compiler_params=_params("arbitrary"),
        name="swa_prompt",
    )(proj, proj, proj, cos, sin, sinks)


def _swa_sample_body(q_ref, k_ref, v_ref, ck_ref, cv_ref, cos_ref, sin_ref, sink_ref, o_ref, krot_ref,
                     *, SB, L, WC):
    cos, sin = cos_ref[...], sin_ref[...]
    qr = _rope(q_ref[...], cos, sin)
    kr = _rope(k_ref[...], cos, sin)
    v = v_ref[...]
    krot_ref[...] = kr
    R = GROUP * L
    rr = lax.broadcasted_iota(jnp.int32, (R, WC + L), 0)
    j = lax.broadcasted_iota(jnp.int32, (R, WC + L), 1)
    rel = rr % L + WC - j
    mask = (rel >= 0) & (rel < WINDOW)
    hrow = lax.broadcasted_iota(jnp.int32, (R, 1), 0) // L
    for b in range(SB):
        rows = slice(b * L, (b + 1) * L)
        for g in range(ATT_KV_HEADS):
            cols = slice(g * HEAD_DIM, (g + 1) * HEAD_DIM)
            keys = jnp.concatenate([ck_ref[b][:, cols], kr[rows, cols]], axis=0).astype(BF16)
            vals = jnp.concatenate([cv_ref[b][:, cols], v[rows, cols]], axis=0).astype(BF16)
            qg = jnp.concatenate(
                [qr[rows, (g * GROUP + hh) * HEAD_DIM:(g * GROUP + hh + 1) * HEAD_DIM]
                 for hh in range(GROUP)], axis=0).astype(BF16)
            sink = jnp.zeros((R, 1), F32)
            for hh in range(GROUP):
                sink = jnp.where(hrow == hh, sink_ref[g * GROUP + hh], sink)
            s = lax.dot_general(qg, keys, (((1,), (1,)), ((), ())),
                                preferred_element_type=F32) * (HEAD_DIM ** -0.5)
            s = jnp.where(mask, s, -jnp.inf)
            o = _sink_softmax_pv(s, sink, vals)
            for hh in range(GROUP):
                h = g * GROUP + hh
                o_ref[rows, h * HEAD_DIM:(h + 1) * HEAD_DIM] = o[hh * L:(hh + 1) * L]


def _swa_sample(proj, cache_k, cache_v, cos, sin, sinks, L, SB):
    rows = proj.shape[0]
    Bd, WC, _ = cache_k.shape
    blk = SB * L
    return pl.pallas_call(
        functools.partial(_swa_sample_body, SB=SB, L=L, WC=WC),
        grid=(Bd // SB,),
        in_specs=[pl.BlockSpec((blk, D_MODEL), lambda i: (i, COL_AQ // D_MODEL)),
                  pl.BlockSpec((blk, KV_WIDTH), lambda i: (i, COL_AK // KV_WIDTH)),
                  pl.BlockSpec((blk, KV_WIDTH), lambda i: (i, COL_AV // KV_WIDTH)),
                  pl.BlockSpec((SB, WC, KV_WIDTH), lambda i: (i, 0, 0)),
                  pl.BlockSpec((SB, WC, KV_WIDTH), lambda i: (i, 0, 0)),
                  pl.BlockSpec((blk, LANES), lambda i: (0, 0)),
                  pl.BlockSpec((blk, LANES), lambda i: (0, 0)),
                  pl.BlockSpec(memory_space=pltpu.SMEM)],
        out_specs=[pl.BlockSpec((blk, D_MODEL), lambda i: (i, 0)),
                   pl.BlockSpec((blk, KV_WIDTH), lambda i: (i, 0))],
        out_shape=[jax.ShapeDtypeStruct((rows, D_MODEL), F32),
                   jax.ShapeDtypeStruct((rows, KV_WIDTH), F32)],
        compiler_params=_params("parallel"),
        name="swa_sample",
    )(proj, proj, proj, cache_k, cache_v, cos, sin, sinks)


def _merge_body(h_ref, oa_ref, ob_ref, ga_ref, gb_ref, w_ref, o_ref):
    m = _sigmoid(ga_ref[...]) * oa_ref[...] + _sigmoid(gb_ref[...]) * ob_ref[...]
    o_ref[...] = h_ref[...] + jnp.dot(m.astype(BF16), w_ref[...], preferred_element_type=F32)


def _merge(h, o_a, o_b, proj, w_out_b, tm):
    rows = h.shape[0]
    row = lambda j: pl.BlockSpec((tm, D_MODEL), lambda i, j=j: (i, j))
    return pl.pallas_call(
        _merge_body,
        grid=(rows // tm,),
        in_specs=[row(0), row(0), row(0), row(COL_GA // D_MODEL), row(COL_GB // D_MODEL),
                  pl.BlockSpec((D_MODEL, D_MODEL), lambda i: (0, 0))],
        out_specs=row(0),
        out_shape=jax.ShapeDtypeStruct((rows, D_MODEL), F32),
        compiler_params=_params("parallel"),
        name="merge",
    )(h, o_a, o_b, proj, proj, w_out_b)


def _top_values(x, n):
    vals = []
    for _ in range(n):
        m = jnp.max(x, axis=0, keepdims=True)
        vals.append(m)
        x = jnp.where(x == m, -jnp.inf, x)
    return jnp.concatenate(vals, axis=0)


def _staircase_sums(t1, t2):
    K = PEER_TOPK
    sub = lax.broadcasted_iota(jnp.int32, (8, t1.shape[1]), 0)
    pieces = [t1[0:1, :] + t2, t1[1:2, :] + t2[0:8, :]]
    for i in range(2, 8):
        pieces.append(jnp.where(sub < K // (i + 1), t1[i:i + 1, :] + t2[0:8, :], -jnp.inf))
    pieces.append(t1[8:16, :] + t2[0:1, :])
    return jnp.concatenate(pieces, axis=0)


def _peer_body(h_ref, nf_ref, nfin_ref, wq_ref, sub_ref, u0_ref, un_ref, vt_ref, o_ref,
               xnT, s1T, s2T, e1T, e2T, tau, act_s, yT, *, TB, EC):
    e = pl.program_id(1)
    K = PEER_TOPK
    slot = e % 2

    @pl.when(e == 0)
    def _():
        xt = _rms(h_ref[...], nf_ref[...]).T.astype(BF16)
        xnT[...] = xt
        act_s[0] = jnp.dot(u0_ref[...], xt, preferred_element_type=F32)
        for h in range(PEER_HEADS):
            tops = []
            scores = []
            for p in range(2):
                hp = 2 * h + p
                qhp = jnp.dot(wq_ref[hp * LANES:(hp + 1) * LANES, :], xt, preferred_element_type=F32)
                s = jnp.dot(sub_ref[hp], qhp.astype(BF16), preferred_element_type=F32)
                scores.append(s)
                tops.append(_top_values(s, K))
            t1, t2 = tops
            cand = _staircase_sums(t1, t2)
            c = _top_values(cand, K)
            z = jnp.sum(jnp.exp(c - c[0:1, :]), axis=0, keepdims=True)
            s1T[h] = scores[0]
            s2T[h] = scores[1]
            e1T[h] = jnp.exp(scores[0] - t1[0:1, :]) / z
            e2T[h] = jnp.exp(scores[1] - t2[0:1, :])
            tau[h:h + 1, :] = c[K - 1:K, :]

    act_next = jnp.dot(un_ref[...], xnT[...], preferred_element_type=F32)
    SLAB = 2 * PEER_NKEYS
    contrib = None
    for j in range(EC // SLAB):
        parts = []
        for al in range(2 * j, 2 * j + 2):
            a = e * (EC // PEER_NKEYS) + al
            w = jnp.zeros((PEER_NKEYS, TB), F32)
            for h in range(PEER_HEADS):
                sel = (s1T[h, pl.ds(a, 1), :] + s2T[h]) >= tau[h:h + 1, :]
                w = w + jnp.where(sel, e1T[h, pl.ds(a, 1), :] * e2T[h], 0.0)
            x = act_s[slot, al * PEER_NKEYS:(al + 1) * PEER_NKEYS, :]
            gelu = 0.5 * x * (1.0 + lax.erf(x * (2.0 ** -0.5)))
            parts.append(w * gelu)
        g = jnp.concatenate(parts, axis=0).astype(BF16)
        d = jnp.dot(vt_ref[:, j * SLAB:(j + 1) * SLAB], g, preferred_element_type=F32)
        contrib = d if contrib is None else contrib + d
    act_s[1 - slot] = act_next

    @pl.when(e == 0)
    def _():
        yT[...] = contrib

    @pl.when(e > 0)
    def _():
        yT[...] += contrib

    @pl.when(e == pl.num_programs(1) - 1)
    def _():
        h3 = h_ref[...] + yT[...].T
        o_ref[...] = _rms(h3, nfin_ref[...])


def _peer(h2, row0, rows, nf, nfin, wqT_b, sub_b, u_b, vT_b, TB, EC):
    nexp = u_b.shape[0]
    ne = nexp // EC
    off = row0 // TB
    f = pl.pallas_call(
        functools.partial(_peer_body, TB=TB, EC=EC),
        grid=(rows // TB, ne),
        in_specs=[pl.BlockSpec((TB, D_MODEL), lambda i, e: (i + off, 0)),
                  pl.BlockSpec((1, D_MODEL), lambda i, e: (0, 0)),
                  pl.BlockSpec((1, D_MODEL), lambda i, e: (0, 0)),
                  pl.BlockSpec(wqT_b.shape, lambda i, e: (0, 0)),
                  pl.BlockSpec(sub_b.shape, lambda i, e: (0, 0, 0)),
                  pl.BlockSpec((EC, D_MODEL), lambda i, e: (0, 0)),
                  pl.BlockSpec((EC, D_MODEL), lambda i, e: (jnp.minimum(e + 1, ne - 1), 0)),
                  pl.BlockSpec((D_MODEL, EC), lambda i, e: (0, e))],
        out_specs=pl.BlockSpec((TB, D_MODEL), lambda i, e: (i, 0)),
        out_shape=jax.ShapeDtypeStruct((rows, D_MODEL), F32),
        scratch_shapes=[pltpu.VMEM((D_MODEL, TB), BF16)]
                       + [pltpu.VMEM((PEER_HEADS, PEER_NKEYS, TB), F32)] * 4
                       + [pltpu.VMEM((PEER_HEADS, TB), F32),
                          pltpu.VMEM((2, EC, TB), F32),
                          pltpu.VMEM((D_MODEL, TB), F32)],
        compiler_params=_params("parallel", "arbitrary"),
        name="peer",
    )
    return f(h2, nf, nfin, wqT_b, sub_b, u_b, u_b, vT_b)


PEER_TB = 256
PEER_EC = 1024
PROJ_TN = 1280
HGRN_CHUNK = 32
SAMPLE_SB = 8


def _row_tile(rows, cap):
    best = 8
    for t in range(8, cap + 1, 8):
        if rows % t == 0:
            best = t
    return best


def kernel(x_prompt, x_sample, state_hgrn, cache_swa_k, cache_swa_v, meta_tokens, w_in, hgrn_lb,
           hgrn_onorm, attn_sinks, w_out, norm_mix, norm_ffn, peer_query, peer_subkeys, peer_u, peer_v,
           norm_final):
    assert w_in.shape[0] == 1, "single-layer trunk"
    B, SEQ, _ = x_prompt.shape
    Bd, L, _ = x_sample.shape
    assert B == 1
    past = PAST_LEN
    wc = cache_swa_k.shape[2]
    START = PEER_TB
    T = START + SEQ
    first_valid = START - N_META

    sizes = np.cumsum([0, 1024, 1024, 1024, 1024, 1024, 256, 256, 1024, 1024])
    seg = lambda i: w_in[0][:, sizes[i]:sizes[i + 1]]
    w_in_b = jnp.concatenate([seg(0), seg(1), seg(2), seg(3), seg(4), seg(7), seg(8), seg(5), seg(6)],
                             axis=1).astype(BF16)
    w_out_b = w_out[0].astype(BF16)
    wqT_b = peer_query[0].T.astype(BF16)
    sub_b = peer_subkeys[0].reshape(PEER_HEADS * 2, PEER_NKEYS, -1).astype(BF16)
    u_b = peer_u[0].astype(BF16)
    vT_b = peer_v[0].T.astype(BF16)
    nmix = norm_mix[0].reshape(1, D_MODEL)
    nffn = norm_ffn[0].reshape(1, D_MODEL)
    nfin = norm_final.reshape(1, D_MODEL)
    onorm = hgrn_onorm[0].reshape(1, HG_DV)
    sinks = attn_sinks[0]

    hp = jnp.concatenate([jnp.zeros((first_valid, D_MODEL), F32), meta_tokens.astype(F32), x_prompt[0]],
                         axis=0)
    proj_p = _norm_proj(hp, nmix, w_in_b, _row_tile(T, 1280), PROJ_TN)
    oa_p, st_p = _hgrn(proj_p[None], hgrn_lb, onorm,
                       jnp.zeros((1, HG_HEADS, HG_DK, HG_DV), F32), HGRN_CHUNK)
    cos_p, sin_p = _rope_tables(jnp.arange(T) - first_valid)
    ob_p, krot_p = _swa_prompt(proj_p, cos_p, sin_p, sinks, first_valid)
    h2_p = _merge(hp, oa_p[0], ob_p, proj_p, w_out_b, _row_tile(T, 512))
    y_prompt = _peer(h2_p, START, SEQ, nffn, nfin, wqT_b, sub_b, u_b, vT_b, PEER_TB, PEER_EC)[None]
    wp = min(WINDOW, SEQ + N_META)
    kp = krot_p[T - wp:].reshape(1, 1, wp, ATT_KV_HEADS, HEAD_DIM)
    vp = proj_p[T - wp:, COL_AV:COL_AV + KV_WIDTH].reshape(1, 1, wp, ATT_KV_HEADS, HEAD_DIM)

    hs = x_sample.reshape(Bd * L, D_MODEL)
    proj_s = _norm_proj(hs, nmix, w_in_b, _row_tile(Bd * L, 512), PROJ_TN)
    oa_s, st_s = _hgrn(proj_s.reshape(Bd, L, PROJ_COLS), hgrn_lb, onorm, state_hgrn[0], L)
    cos_s, sin_s = _rope_tables(past + jnp.arange(L))
    cos_s, sin_s = jnp.tile(cos_s, (SAMPLE_SB, 1)), jnp.tile(sin_s, (SAMPLE_SB, 1))
    ck = cache_swa_k[0].reshape(Bd, wc, KV_WIDTH)
    cv = cache_swa_v[0].reshape(Bd, wc, KV_WIDTH)
    ob_s, krot_s = _swa_sample(proj_s, ck, cv, cos_s, sin_s, sinks, L, SAMPLE_SB)
    h2_s = _merge(hs, oa_s.reshape(Bd * L, D_MODEL), ob_s, proj_s, w_out_b, _row_tile(Bd * L, 512))
    y_sample = _peer(h2_s, 0, Bd * L, nffn, nfin, wqT_b, sub_b, u_b, vT_b, PEER_TB, PEER_EC)
    y_sample = y_sample.reshape(Bd, L, D_MODEL)
    knew = krot_s.reshape(Bd, L, KV_WIDTH)
    vnew = proj_s[:, COL_AV:COL_AV + KV_WIDTH].reshape(Bd, L, KV_WIDTH)
    ks = jnp.concatenate([ck, knew], axis=1)[:, L:].reshape(1, Bd, wc, ATT_KV_HEADS, HEAD_DIM)
    vs = jnp.concatenate([cv, vnew], axis=1)[:, L:].reshape(1, Bd, wc, ATT_KV_HEADS, HEAD_DIM)

    return (y_prompt, y_sample, st_p[None], st_s[None], kp, ks, vp, vs)
```

```python
import functools

import jax
import jax.numpy as jnp
import numpy as np
from jax import lax
from jax.experimental import pallas as pl
from jax.experimental.pallas import tpu as pltpu

F32 = jnp.float32
BF16 = jnp.bfloat16

D_MODEL = 1024
N_META = 16
HG_HEADS = 8
HG_DK = 128
HG_DV = 128
ATT_HEADS = 16
ATT_KV_HEADS = 4
GROUP = ATT_HEADS // ATT_KV_HEADS
HEAD_DIM = 64
KV_WIDTH = ATT_KV_HEADS * HEAD_DIM
WINDOW = 128
ROPE_THETA = 10000.0
PAST_LEN = 16384
PEER_HEADS = 8
PEER_NKEYS = 128
PEER_TOPK = 16
EPS = 1e-6

LANES = 128
VMEM_LIMIT = 56 * 1024 * 1024

COL_PQ, COL_PF, COL_PI, COL_PG, COL_AQ, COL_GA, COL_GB = (i * D_MODEL for i in range(7))
COL_AK = 7 * D_MODEL
COL_AV = COL_AK + KV_WIDTH
PROJ_COLS = COL_AV + KV_WIDTH


def _sigmoid(x):
    return jax.nn.sigmoid(x)


def _rms(x, w):
    return x * lax.rsqrt(jnp.mean(x * x, axis=-1, keepdims=True) + EPS) * w


def _params(*sem):
    return pltpu.CompilerParams(dimension_semantics=sem, vmem_limit_bytes=VMEM_LIMIT)


def _proj_body(x_ref, nw_ref, w_ref, o_ref, xn_ref):
    @pl.when(pl.program_id(1) == 0)
    def _():
        xn_ref[...] = _rms(x_ref[...], nw_ref[...]).astype(BF16)

    o_ref[...] = jnp.dot(xn_ref[...], w_ref[...], preferred_element_type=F32)


def _norm_proj(x, nw, w_b, tm, tn):
    rows, n = x.shape[0], w_b.shape[1]
    return pl.pallas_call(
        _proj_body,
        grid=(rows // tm, n // tn),
        in_specs=[pl.BlockSpec((tm, D_MODEL), lambda i, j: (i, 0)),
                  pl.BlockSpec((1, D_MODEL), lambda i, j: (0, 0)),
                  pl.BlockSpec((D_MODEL, tn), lambda i, j: (0, j))],
        out_specs=pl.BlockSpec((tm, tn), lambda i, j: (i, j)),
        out_shape=jax.ShapeDtypeStruct((rows, n), F32),
        scratch_shapes=[pltpu.VMEM((tm, D_MODEL), BF16)],
        compiler_params=_params("parallel", "arbitrary"),
        name="norm_proj",
    )(x, nw, w_b)


def _pad_rows(x, rows):
    if x.shape[0] >= rows:
        return x
    return jnp.concatenate([x, jnp.zeros((rows - x.shape[0], x.shape[1]), x.dtype)], axis=0)


def _hgrn_body(pq_ref, pf_ref, pi_ref, pg_ref, lb_ref, on_ref, s0_ref, o_ref, sout_ref, ST, *, C):
    c = pl.program_id(1)
    W2 = 2 * HG_DK
    NG = C // 8

    @pl.when(c == 0)
    def _():
        for h in range(HG_HEADS):
            ST[h] = s0_ref[h].T

    lbr = lb_ref[...]
    lbe = jnp.exp(lbr - jnp.max(lbr, axis=0, keepdims=True))
    lb = lbe[0:1] / jnp.sum(lbe, axis=0, keepdims=True)

    pq = pq_ref[...]
    pg = pg_ref[...]
    q = pq * _sigmoid(pq)
    f = lb + (1.0 - lb) * _sigmoid(pf_ref[...])
    k = 1.0 - f
    v = pi_ref[...]

    row = lax.broadcasted_iota(jnp.int32, (C, HG_HEADS * HG_DK), 0)
    b = jnp.log(f)
    sh = 1
    while sh < C:
        b = b + jnp.where(row >= sh, pltpu.roll(b, sh, axis=0), 0.0)
        sh *= 2
    blast = b[C - 1:C, :]
    qt = q * jnp.exp(b)
    kt = k * jnp.exp(blast - b)
    eblast = jnp.exp(blast)

    r2 = lax.broadcasted_iota(jnp.int32, (W2, W2), 0) // HG_DK
    c2 = lax.broadcasted_iota(jnp.int32, (W2, W2), 1) // HG_DK
    ones_blk = (r2 == c2).astype(BF16)
    acc = [[jnp.zeros((8, W2), F32) for _ in range(NG)] for _ in range(HG_HEADS // 2)]
    for sg in range(NG):
        t0 = 8 * sg
        n = C - t0
        bt, qq = b[t0:, :], q[t0:, :]
        trow = lax.broadcasted_iota(jnp.int32, (n, HG_HEADS * HG_DK), 0) + t0
        pieces = []
        for s in range(t0, t0 + 8):
            e = jnp.exp(jnp.where(trow >= s, bt - b[s:s + 1, :], -jnp.inf))
            pieces.append(e * (qq * k[s:s + 1, :]))
        E = jnp.concatenate(pieces, axis=0).astype(BF16)
        for hp in range(HG_HEADS // 2):
            lanes = slice(hp * W2, (hp + 1) * W2)
            R = jnp.dot(E[:, lanes], ones_blk, preferred_element_type=F32)
            for si in range(8):
                vrow = v[t0 + si:t0 + si + 1, lanes]
                for tg in range(n // 8):
                    blk = R[si * n + 8 * tg:si * n + 8 * tg + 8, :]
                    acc[hp][sg + tg] = acc[hp][sg + tg] + blk * vrow
    o_intra = jnp.concatenate([jnp.concatenate(a, axis=0) for a in acc], axis=1)

    onw = on_ref[...]
    qtb = _pad_rows(qt, 16).astype(BF16)
    ktb = _pad_rows(kt, 16).astype(BF16)
    vb = _pad_rows(v, 16).astype(BF16)
    for h in range(HG_HEADS):
        sl = slice(h * HG_DK, (h + 1) * HG_DK)
        Sh = ST[h]
        o_h = lax.dot_general(qtb[:, sl], Sh.astype(BF16), (((1,), (1,)), ((), ())),
                              preferred_element_type=F32)[:C] + o_intra[:, sl]
        on = o_h * lax.rsqrt(jnp.mean(o_h * o_h, axis=-1, keepdims=True) + EPS) * onw
        g = pg[:, sl]
        o_ref[:, sl] = on * (g * _sigmoid(g))
        upd = lax.dot_general(vb[:, sl], ktb[:, sl], (((0,), (0,)), ((), ())),
                              preferred_element_type=F32)
        ST[h] = Sh * eblast[:, sl] + upd

    @pl.when(c == pl.num_programs(1) - 1)
    def _():
        for h in range(HG_HEADS):
            sout_ref[h] = ST[h].T


def _hgrn(proj, lb, onorm, state, chunk):
    B, T, _ = proj.shape
    width = HG_HEADS * HG_DK
    col = lambda j: pl.BlockSpec((None, chunk, width), lambda b, c, j=j: (b, c, j))
    st = pl.BlockSpec((None, HG_HEADS, HG_DK, HG_DV), lambda b, c: (b, 0, 0, 0))
    return pl.pallas_call(
        functools.partial(_hgrn_body, C=chunk),
        grid=(B, T // chunk),
        in_specs=[col(COL_PQ // width), col(COL_PF // width), col(COL_PI // width), col(COL_PG // width),
                  pl.BlockSpec(lb.shape, lambda b, c: (0, 0)),
                  pl.BlockSpec((1, HG_DV), lambda b, c: (0, 0)),
                  st],
        out_specs=[pl.BlockSpec((None, chunk, width), lambda b, c: (b, c, 0)), st],
        out_shape=[jax.ShapeDtypeStruct((B, T, width), F32),
                   jax.ShapeDtypeStruct(state.shape, F32)],
        scratch_shapes=[pltpu.VMEM((HG_HEADS, HG_DV, HG_DK), F32)],
        compiler_params=_params("parallel", "arbitrary"),
        name="hgrn",
    )(proj, proj, proj, proj, lb, onorm, state)


def _rope_tables(pos):
    half = HEAD_DIM // 2
    inv = ROPE_THETA ** (-jnp.arange(half, dtype=F32) / half)
    ang = pos.astype(F32)[:, None] * inv[None, :]
    cos, sin = jnp.cos(ang), jnp.sin(ang)
    cos = jnp.concatenate([cos, cos, cos, cos], axis=1)
    sin = jnp.concatenate([-sin, sin, -sin, sin], axis=1)
    return cos, sin


def _rope(x, cos, sin):
    lane = lax.broadcasted_iota(jnp.int32, (x.shape[0], LANES), 1)
    first = (lane % HEAD_DIM) < (HEAD_DIM // 2)
    outs = []
    for s in range(x.shape[1] // LANES):
        xs = x[:, s * LANES:(s + 1) * LANES]
        partner = jnp.where(first, pltpu.roll(xs, LANES - HEAD_DIM // 2, axis=1),
                            pltpu.roll(xs, HEAD_DIM // 2, axis=1))
        outs.append(xs * cos + partner * sin)
    return jnp.concatenate(outs, axis=1)


def _sink_softmax_pv(s, sink, vg):
    m = jnp.maximum(jnp.max(s, axis=-1, keepdims=True), sink)
    p = jnp.exp(s - m)
    den = jnp.sum(p, axis=-1, keepdims=True) + jnp.exp(sink - m)
    return jnp.dot(p.astype(BF16), vg, preferred_element_type=F32) / den


def _swa_prompt_body(q_ref, k_ref, v_ref, cos_ref, sin_ref, sink_ref, o_ref, krot_ref, kprev, vprev,
                     *, first_valid):
    i = pl.program_id(0)

    @pl.when(i == 0)
    def _():
        kprev[...] = jnp.zeros_like(kprev)
        vprev[...] = jnp.zeros_like(vprev)

    cos, sin = cos_ref[...], sin_ref[...]
    qr = _rope(q_ref[...], cos, sin)
    kr = _rope(k_ref[...], cos, sin)
    v = v_ref[...]
    krot_ref[...] = kr
    kk = jnp.concatenate([kprev[...], kr], axis=0).astype(BF16)
    vv = jnp.concatenate([vprev[...], v], axis=0).astype(BF16)
    R = GROUP * WINDOW
    r = lax.broadcasted_iota(jnp.int32, (R, 2 * WINDOW), 0) % WINDOW
    j = lax.broadcasted_iota(jnp.int32, (R, 2 * WINDOW), 1)
    mask = (j > r) & (j <= r + WINDOW) & ((i - 1) * WINDOW + j >= first_valid)
    hrow = lax.broadcasted_iota(jnp.int32, (R, 1), 0) // WINDOW
    for g in range(ATT_KV_HEADS):
        kg = kk[:, g * HEAD_DIM:(g + 1) * HEAD_DIM]
        vg = vv[:, g * HEAD_DIM:(g + 1) * HEAD_DIM]
        qg = jnp.concatenate([qr[:, (g * GROUP + hh) * HEAD_DIM:(g * GROUP + hh + 1) * HEAD_DIM]
                              for hh in range(GROUP)], axis=0).astype(BF16)
        sink = jnp.zeros((R, 1), F32)
        for hh in range(GROUP):
            sink = jnp.where(hrow == hh, sink_ref[g * GROUP + hh], sink)
        s = lax.dot_general(qg, kg, (((1,), (1,)), ((), ())),
                            preferred_element_type=F32) * (HEAD_DIM ** -0.5)
        o = _sink_softmax_pv(jnp.where(mask, s, -jnp.inf), sink, vg)
        for hh in range(GROUP):
            h = g * GROUP + hh
            o_ref[:, h * HEAD_DIM:(h + 1) * HEAD_DIM] = o[hh * WINDOW:(hh + 1) * WINDOW]
    kprev[...] = kr
    vprev[...] = v


def _swa_prompt(proj, cos, sin, sinks, first_valid):
    T = proj.shape[0]
    nb = T // WINDOW
    return pl.pallas_call(
        functools.partial(_swa_prompt_body, first_valid=first_valid),
        grid=(nb,),
        in_specs=[pl.BlockSpec((WINDOW, D_MODEL), lambda i: (i, COL_AQ // D_MODEL)),
                  pl.BlockSpec((WINDOW, KV_WIDTH), lambda i: (i, COL_AK // KV_WIDTH)),
                  pl.BlockSpec((WINDOW, KV_WIDTH), lambda i: (i, COL_AV // KV_WIDTH)),
                  pl.BlockSpec((WINDOW, LANES), lambda i: (i, 0)),
                  pl.BlockSpec((WINDOW, LANES), lambda i: (i, 0)),
                  pl.BlockSpec(memory_space=pltpu.SMEM)],
        out_specs=[pl.BlockSpec((WINDOW, D_MODEL), lambda i: (i, 0)),
                   pl.BlockSpec((WINDOW, KV_WIDTH), lambda i: (i, 0))],
        out_shape=[jax.ShapeDtypeStruct((T, D_MODEL), F32),
                   jax.ShapeDtypeStruct((T, KV_WIDTH), F32)],
        scratch_shapes=[pltpu.VMEM((WINDOW, KV_WIDTH), F32)] * 2,
        compiler_params=_params("arbitrary"),
        name="swa_prompt",
    )(proj, proj, proj, cos, sin, sinks)


def _swa_sample_body(q_ref, k_ref, v_ref, ck_ref, cv_ref, cos_ref, sin_ref, sink_ref, o_ref, krot_ref,
                     *, SB, L, WC):
    cos, sin = cos_ref[...], sin_ref[...]
    qr = _rope(q_ref[...], cos, sin)
    kr = _rope(k_ref[...], cos, sin)
    v = v_ref[...]
    krot_ref[...] = kr
    R = GROUP * L
    rr = lax.broadcasted_iota(jnp.int32, (R, WC + L), 0)
    j = lax.broadcasted_iota(jnp.int32, (R, WC + L), 1)
    rel = rr % L + WC - j
    mask = (rel >= 0) & (rel < WINDOW)
    hrow = lax.broadcasted_iota(jnp.int32, (R, 1), 0) // L
    for b in range(SB):
        rows = slice(b * L, (b + 1) * L)
        for g in range(ATT_KV_HEADS):
            cols = slice(g * HEAD_DIM, (g + 1) * HEAD_DIM)
            keys = jnp.concatenate([ck_ref[b][:, cols], kr[rows, cols]], axis=0).astype(BF16)
            vals = jnp.concatenate([cv_ref[b][:, cols], v[rows, cols]], axis=0).astype(BF16)
            qg = jnp.concatenate(
                [qr[rows, (g * GROUP + hh) * HEAD_DIM:(g * GROUP + hh + 1) * HEAD_DIM]
                 for hh in range(GROUP)], axis=0).astype(BF16)
            sink = jnp.zeros((R, 1), F32)
            for hh in range(GROUP):
                sink = jnp.where(hrow == hh, sink_ref[g * GROUP + hh], sink)
            s = lax.dot_general(qg, keys, (((1,), (1,)), ((), ())),
                                preferred_element_type=F32) * (HEAD_DIM ** -0.5)
            s = jnp.where(mask, s, -jnp.inf)
            o = _sink_softmax_pv(s, sink, vals)
            for hh in range(GROUP):
                h = g * GROUP + hh
                o_ref[rows, h * HEAD_DIM:(h + 1) * HEAD_DIM] = o[hh * L:(hh + 1) * L]


def _swa_sample(proj, cache_k, cache_v, cos, sin, sinks, L, SB):
    rows = proj.shape[0]
    Bd, WC, _ = cache_k.shape
    blk = SB * L
    return pl.pallas_call(
        functools.partial(_swa_sample_body, SB=SB, L=L, WC=WC),
        grid=(Bd // SB,),
        in_specs=[pl.BlockSpec((blk, D_MODEL), lambda i: (i, COL_AQ // D_MODEL)),
                  pl.BlockSpec((blk, KV_WIDTH), lambda i: (i, COL_AK // KV_WIDTH)),
                  pl.BlockSpec((blk, KV_WIDTH), lambda i: (i, COL_AV // KV_WIDTH)),
                  pl.BlockSpec((SB, WC, KV_WIDTH), lambda i: (i, 0, 0)),
                  pl.BlockSpec((SB, WC, KV_WIDTH), lambda i: (i, 0, 0)),
                  pl.BlockSpec((blk, LANES), lambda i: (0, 0)),
                  pl.BlockSpec((blk, LANES), lambda i: (0, 0)),
                  pl.BlockSpec(memory_space=pltpu.SMEM)],
        out_specs=[pl.BlockSpec((blk, D_MODEL), lambda i: (i, 0)),
                   pl.BlockSpec((blk, KV_WIDTH), lambda i: (i, 0))],
        out_shape=[jax.ShapeDtypeStruct((rows, D_MODEL), F32),
                   jax.ShapeDtypeStruct((rows, KV_WIDTH), F32)],
        compiler_params=_params("parallel"),
        name="swa_sample",
    )(proj, proj, proj, cache_k, cache_v, cos, sin, sinks)


def _merge_body(h_ref, oa_ref, ob_ref, ga_ref, gb_ref, w_ref, o_ref):
    m = _sigmoid(ga_ref[...]) * oa_ref[...] + _sigmoid(gb_ref[...]) * ob_ref[...]
    o_ref[...] = h_ref[...] + jnp.dot(m.astype(BF16), w_ref[...], preferred_element_type=F32)


def _merge(h, o_a, o_b, proj, w_out_b, tm):
    rows = h.shape[0]
    row = lambda j: pl.BlockSpec((tm, D_MODEL), lambda i, j=j: (i, j))
    return pl.pallas_call(
        _merge_body,
        grid=(rows // tm,),
        in_specs=[row(0), row(0), row(0), row(COL_GA // D_MODEL), row(COL_GB // D_MODEL),
                  pl.BlockSpec((D_MODEL, D_MODEL), lambda i: (0, 0))],
        out_specs=row(0),
        out_shape=jax.ShapeDtypeStruct((rows, D_MODEL), F32),
        compiler_params=_params("parallel"),
        name="merge",
    )(h, o_a, o_b, proj, proj, w_out_b)


def _top_values(x, n, ranked=False):
    vals = []
    rank = jnp.full(x.shape, float(n), F32)
    for it in range(n):
        m = jnp.max(x, axis=0, keepdims=True)
        vals.append(m)
        hit = x == m
        x = jnp.where(hit, -jnp.inf, x)
        if ranked:
            rank = jnp.where(hit, float(it), rank)
    top = jnp.concatenate(vals, axis=0)
    return (top, rank) if ranked else top


def _staircase_sums(t1, t2):
    K = PEER_TOPK
    sub = lax.broadcasted_iota(jnp.int32, (8, t1.shape[1]), 0)
    pieces = [t1[0:1, :] + t2, t1[1:2, :] + t2[0:8, :]]
    for i in range(2, 8):
        pieces.append(jnp.where(sub < K // (i + 1), t1[i:i + 1, :] + t2[0:8, :], -jnp.inf))
    pieces.append(t1[8:16, :] + t2[0:1, :])
    return jnp.concatenate(pieces, axis=0)


def _rows_bf16(row, rows):
    tile = jnp.broadcast_to(row, (16, row.shape[1])).astype(BF16)
    return jnp.concatenate([tile] * (rows // 16), axis=0)


def _peer_body(h_ref, nf_ref, nfin_ref, wq_ref, sub_ref, u0_ref, un_ref, vt_ref, o_ref,
               xnT, m1T, e1T, r2T, e2T, act_s, yT, *, TB, EC):
    e = pl.program_id(1)
    K = PEER_TOPK
    slot = e % 2

    @pl.when(e == 0)
    def _():
        xt = _rms(h_ref[...], nf_ref[...]).T.astype(BF16)
        xnT[...] = xt
        act_s[0] = jnp.dot(u0_ref[...], xt, preferred_element_type=F32)
        for h in range(PEER_HEADS):
            scores = []
            for p in range(2):
                hp = 2 * h + p
                qhp = jnp.dot(wq_ref[hp * LANES:(hp + 1) * LANES, :], xt, preferred_element_type=F32)
                scores.append(jnp.dot(sub_ref[hp], qhp.astype(BF16), preferred_element_type=F32))
            s1, s2 = scores
            t1 = _top_values(s1, K)
            t2, rank2 = _top_values(s2, K, ranked=True)
            c = _top_values(_staircase_sums(t1, t2), K)
            tau = c[K - 1:K, :]
            z = jnp.sum(jnp.exp(c - c[0:1, :]), axis=0, keepdims=True)
            m = jnp.zeros_like(s1)
            for j in range(K):
                m = jnp.where(s1 + t2[j:j + 1, :] >= tau, float(j + 1), m)
            m1T[h] = m
            r2T[h] = rank2.astype(BF16)
            e1T[h] = jnp.exp(s1 - t1[0:1, :]) / z
            e2T[h] = jnp.exp(s2 - t2[0:1, :]).astype(BF16)

    act_next = jnp.dot(un_ref[...], xnT[...], preferred_element_type=F32)
    SLAB = 2 * PEER_NKEYS
    contrib = None
    for j in range(EC // SLAB):
        parts = []
        for al in range(2 * j, 2 * j + 2):
            a = e * (EC // PEER_NKEYS) + al
            w = jnp.zeros((PEER_NKEYS, TB), BF16)
            for h in range(PEER_HEADS):
                sel = r2T[h] < _rows_bf16(m1T[h, pl.ds(a, 1), :], PEER_NKEYS)
                gate = _rows_bf16(e1T[h, pl.ds(a, 1), :], PEER_NKEYS) * e2T[h]
                w = w + jnp.where(sel, gate, jnp.zeros_like(gate))
            x = act_s[slot, al * PEER_NKEYS:(al + 1) * PEER_NKEYS, :]
            gelu = 0.5 * x * (1.0 + lax.erf(x * (2.0 ** -0.5)))
            parts.append(w * gelu.astype(BF16))
        g = jnp.concatenate(parts, axis=0)
        d = jnp.dot(vt_ref[:, j * SLAB:(j + 1) * SLAB], g, preferred_element_type=F32)
        contrib = d if contrib is None else contrib + d
    act_s[1 - slot] = act_next

    @pl.when(e == 0)
    def _():
        yT[...] = contrib

    @pl.when(e > 0)
    def _():
        yT[...] += contrib

    @pl.when(e == pl.num_programs(1) - 1)
    def _():
        h3 = h_ref[...] + yT[...].T
        o_ref[...] = _rms(h3, nfin_ref[...])


def _peer(h2, row0, rows, nf, nfin, wqT_b, sub_b, u_b, vT_b, TB, EC):
    nexp = u_b.shape[0]
    ne = nexp // EC
    off = row0 // TB
    f = pl.pallas_call(
        functools.partial(_peer_body, TB=TB, EC=EC),
        grid=(rows // TB, ne),
        in_specs=[pl.BlockSpec((TB, D_MODEL), lambda i, e: (i + off, 0)),
                  pl.BlockSpec((1, D_MODEL), lambda i, e: (0, 0)),
                  pl.BlockSpec((1, D_MODEL), lambda i, e: (0, 0)),
                  pl.BlockSpec(wqT_b.shape, lambda i, e: (0, 0)),
                  pl.BlockSpec(sub_b.shape, lambda i, e: (0, 0, 0)),
                  pl.BlockSpec((EC, D_MODEL), lambda i, e: (0, 0)),
                  pl.BlockSpec((EC, D_MODEL), lambda i, e: (jnp.minimum(e + 1, ne - 1), 0)),
                  pl.BlockSpec((D_MODEL, EC), lambda i, e: (0, e))],
        out_specs=pl.BlockSpec((TB, D_MODEL), lambda i, e: (i, 0)),
        out_shape=jax.ShapeDtypeStruct((rows, D_MODEL), F32),
        scratch_shapes=[pltpu.VMEM((D_MODEL, TB), BF16),
                        pltpu.VMEM((PEER_HEADS, PEER_NKEYS, TB), F32),
                        pltpu.VMEM((PEER_HEADS, PEER_NKEYS, TB), F32),
                        pltpu.VMEM((PEER_HEADS, PEER_NKEYS, TB), BF16),
                        pltpu.VMEM((PEER_HEADS, PEER_NKEYS, TB), BF16),
                        pltpu.VMEM((2, EC, TB), F32),
                        pltpu.VMEM((D_MODEL, TB), F32)],
        compiler_params=_params("parallel", "arbitrary"),
        name="peer",
    )
    return f(h2, nf, nfin, wqT_b, sub_b, u_b, u_b, vT_b)


PEER_TB = 256
PEER_EC = 1024
PROJ_TN = 1280
HGRN_CHUNK = 32
SAMPLE_SB = 8


def _row_tile(rows, cap):
    best = 8
    for t in range(8, cap + 1, 8):
        if rows % t == 0:
            best = t
    return best


def kernel(x_prompt, x_sample, state_hgrn, cache_swa_k, cache_swa_v, meta_tokens, w_in, hgrn_lb,
           hgrn_onorm, attn_sinks, w_out, norm_mix, norm_ffn, peer_query, peer_subkeys, peer_u, peer_v,
           norm_final):
    assert w_in.shape[0] == 1, "single-layer trunk"
    B, SEQ, _ = x_prompt.shape
    Bd, L, _ = x_sample.shape
    assert B == 1
    past = PAST_LEN
    wc = cache_swa_k.shape[2]
    START = PEER_TB
    T = START + SEQ
    first_valid = START - N_META

    sizes = np.cumsum([0, 1024, 1024, 1024, 1024, 1024, 256, 256, 1024, 1024])
    seg = lambda i: w_in[0][:, sizes[i]:sizes[i + 1]]
    w_in_b = jnp.concatenate([seg(0), seg(1), seg(2), seg(3), seg(4), seg(7), seg(8), seg(5), seg(6)],
                             axis=1).astype(BF16)
    w_out_b = w_out[0].astype(BF16)
    wqT_b = peer_query[0].T.astype(BF16)
    sub_b = peer_subkeys[0].reshape(PEER_HEADS * 2, PEER_NKEYS, -1).astype(BF16)
    u_b = peer_u[0].astype(BF16)
    vT_b = peer_v[0].T.astype(BF16)
    nmix = norm_mix[0].reshape(1, D_MODEL)
    nffn = norm_ffn[0].reshape(1, D_MODEL)
    nfin = norm_final.reshape(1, D_MODEL)
    onorm = hgrn_onorm[0].reshape(1, HG_DV)
    sinks = attn_sinks[0]

    hp = jnp.concatenate([jnp.zeros((first_valid, D_MODEL), F32), meta_tokens.astype(F32), x_prompt[0]],
                         axis=0)
    proj_p = _norm_proj(hp, nmix, w_in_b, _row_tile(T, 1280), PROJ_TN)
    oa_p, st_p = _hgrn(proj_p[None], hgrn_lb, onorm,
                       jnp.zeros((1, HG_HEADS, HG_DK, HG_DV), F32), HGRN_CHUNK)
    cos_p, sin_p = _rope_tables(jnp.arange(T) - first_valid)
    ob_p, krot_p = _swa_prompt(proj_p, cos_p, sin_p, sinks, first_valid)
    h2_p = _merge(hp, oa_p[0], ob_p, proj_p, w_out_b, _row_tile(T, 512))
    y_prompt = _peer(h2_p, START, SEQ, nffn, nfin, wqT_b, sub_b, u_b, vT_b, PEER_TB, PEER_EC)[None]
    wp = min(WINDOW, SEQ + N_META)
    kp = krot_p[T - wp:].reshape(1, 1, wp, ATT_KV_HEADS, HEAD_DIM)
    vp = proj_p[T - wp:, COL_AV:COL_AV + KV_WIDTH].reshape(1, 1, wp, ATT_KV_HEADS, HEAD_DIM)

    hs = x_sample.reshape(Bd * L, D_MODEL)
    proj_s = _norm_proj(hs, nmix, w_in_b, _row_tile(Bd * L, 512), PROJ_TN)
    oa_s, st_s = _hgrn(proj_s.reshape(Bd, L, PROJ_COLS), hgrn_lb, onorm, state_hgrn[0], L)
    cos_s, sin_s = _rope_tables(past + jnp.arange(L))
    cos_s, sin_s = jnp.tile(cos_s, (SAMPLE_SB, 1)), jnp.tile(sin_s, (SAMPLE_SB, 1))
    ck = cache_swa_k[0].reshape(Bd, wc, KV_WIDTH)
    cv = cache_swa_v[0].reshape(Bd, wc, KV_WIDTH)
    ob_s, krot_s = _swa_sample(proj_s, ck, cv, cos_s, sin_s, sinks, L, SAMPLE_SB)
    h2_s = _merge(hs, oa_s.reshape(Bd * L, D_MODEL), ob_s, proj_s, w_out_b, _row_tile(Bd * L, 512))
    y_sample = _peer(h2_s, 0, Bd * L, nffn, nfin, wqT_b, sub_b, u_b, vT_b, PEER_TB, PEER_EC)
    y_sample = y_sample.reshape(Bd, L, D_MODEL)
    knew = krot_s.reshape(Bd, L, KV_WIDTH)
    vnew = proj_s[:, COL_AV:COL_AV + KV_WIDTH].reshape(Bd, L, KV_WIDTH)
    ks = jnp.concatenate([ck, knew], axis=1)[:, L:].reshape(1, Bd, wc, ATT_KV_HEADS, HEAD_DIM)
    vs = jnp.concatenate([cv, vnew], axis=1)[:, L:].reshape(1, Bd, wc, ATT_KV_HEADS, HEAD_DIM)

    return (y_prompt, y_sample, st_p[None], st_s[None], kp, ks, vp, vs)
```

```python
import functools

import jax
import jax.numpy as jnp
import numpy as np
from jax import lax
from jax.experimental import pallas as pl
from jax.experimental.pallas import tpu as pltpu

F32 = jnp.float32
BF16 = jnp.bfloat16

D_MODEL = 1024
N_META = 16
HG_HEADS = 8
HG_DK = 128
HG_DV = 128
ATT_HEADS = 16
ATT_KV_HEADS = 4
GROUP = ATT_HEADS // ATT_KV_HEADS
HEAD_DIM = 64
KV_WIDTH = ATT_KV_HEADS * HEAD_DIM
WINDOW = 128
ROPE_THETA = 10000.0
PAST_LEN = 16384
PEER_HEADS = 8
PEER_NKEYS = 128
PEER_TOPK = 16
EPS = 1e-6
LOG2E = 1.4426950408889634

LANES = 128
VMEM_LIMIT = 56 * 1024 * 1024

COL_PQ, COL_PF, COL_PI, COL_PG, COL_AQ, COL_GA, COL_GB = (i * D_MODEL for i in range(7))
COL_AK = 7 * D_MODEL
COL_AV = COL_AK + KV_WIDTH
PROJ_COLS = COL_AV + KV_WIDTH


def _sigmoid(x):
    return jax.nn.sigmoid(x)


def _rms(x, w):
    return x * lax.rsqrt(jnp.mean(x * x, axis=-1, keepdims=True) + EPS) * w


def _params(*sem):
    return pltpu.CompilerParams(dimension_semantics=sem, vmem_limit_bytes=VMEM_LIMIT)


def _proj_body(x_ref, nw_ref, w_ref, o_ref, xn_ref):
    @pl.when(pl.program_id(1) == 0)
    def _():
        xn_ref[...] = _rms(x_ref[...], nw_ref[...]).astype(BF16)

    o_ref[...] = jnp.dot(xn_ref[...], w_ref[...], preferred_element_type=F32)


def _norm_proj(x, nw, w_b, tm, tn):
    rows, n = x.shape[0], w_b.shape[1]
    return pl.pallas_call(
        _proj_body,
        grid=(rows // tm, n // tn),
        in_specs=[pl.BlockSpec((tm, D_MODEL), lambda i, j: (i, 0)),
                  pl.BlockSpec((1, D_MODEL), lambda i, j: (0, 0)),
                  pl.BlockSpec((D_MODEL, tn), lambda i, j: (0, j))],
        out_specs=pl.BlockSpec((tm, tn), lambda i, j: (i, j)),
        out_shape=jax.ShapeDtypeStruct((rows, n), F32),
        scratch_shapes=[pltpu.VMEM((tm, D_MODEL), BF16)],
        compiler_params=_params("parallel", "arbitrary"),
        name="norm_proj",
    )(x, nw, w_b)


def _pad_rows(x, rows):
    if x.shape[0] >= rows:
        return x
    return jnp.concatenate([x, jnp.zeros((rows - x.shape[0], x.shape[1]), x.dtype)], axis=0)


def _hgrn_body(pq_ref, pf_ref, pi_ref, pg_ref, lb_ref, on_ref, s0_ref, o_ref, sout_ref, ST, *, C, BB):
    for bi in range(BB):
        _hgrn_one(pq_ref.at[bi], pf_ref.at[bi], pi_ref.at[bi], pg_ref.at[bi], lb_ref, on_ref,
                  s0_ref.at[bi], o_ref.at[bi], sout_ref.at[bi], ST.at[bi], C=C)


def _hgrn_one(pq_ref, pf_ref, pi_ref, pg_ref, lb_ref, on_ref, s0_ref, o_ref, sout_ref, ST, *, C):
    c = pl.program_id(1)
    W2 = 2 * HG_DK
    NG = C // 8

    @pl.when(c == 0)
    def _():
        for h in range(HG_HEADS):
            ST[h] = s0_ref[h].T

    lbr = lb_ref[...]
    lbe = jnp.exp(lbr - jnp.max(lbr, axis=0, keepdims=True))
    lb = lbe[0:1] / jnp.sum(lbe, axis=0, keepdims=True)

    pq = pq_ref[...]
    pg = pg_ref[...]
    q = pq * _sigmoid(pq)
    f = lb + (1.0 - lb) * _sigmoid(pf_ref[...])
    k = 1.0 - f
    v = pi_ref[...]

    row = lax.broadcasted_iota(jnp.int32, (C, HG_HEADS * HG_DK), 0)
    b = jnp.log(f)
    sh = 1
    while sh < C:
        b = b + jnp.where(row >= sh, pltpu.roll(b, sh, axis=0), 0.0)
        sh *= 2
    blast = b[C - 1:C, :]
    qt = q * jnp.exp(b)
    kt = k * jnp.exp(blast - b)
    eblast = jnp.exp(blast)

    r2 = lax.broadcasted_iota(jnp.int32, (W2, W2), 0) // HG_DK
    c2 = lax.broadcasted_iota(jnp.int32, (W2, W2), 1) // HG_DK
    ones_blk = (r2 == c2).astype(BF16)
    b2 = b * LOG2E
    cs = b2 - jnp.log2(k)
    sub8 = lax.broadcasted_iota(jnp.int32, (8, HG_HEADS * HG_DK), 0)
    acc = [[jnp.zeros((8, W2), F32) for _ in range(NG)] for _ in range(HG_HEADS // 2)]
    for sg in range(NG):
        t0 = 8 * sg
        n = C - t0
        bt, qq = b2[t0:, :], q[t0:, :]
        pieces = []
        for s in range(t0, t0 + 8):
            d = bt - cs[s:s + 1, :]
            d = jnp.concatenate([jnp.where(sub8 >= s - t0, d[:8], -jnp.inf), d[8:]], axis=0) if n > 8 \
                else jnp.where(sub8 >= s - t0, d, -jnp.inf)
            pieces.append(jnp.exp2(d) * qq)
        E = jnp.concatenate(pieces, axis=0).astype(BF16)
        for hp in range(HG_HEADS // 2):
            lanes = slice(hp * W2, (hp + 1) * W2)
            R = jnp.dot(E[:, lanes], ones_blk, preferred_element_type=F32)
            for si in range(8):
                vrow = v[t0 + si:t0 + si + 1, lanes]
                for tg in range(n // 8):
                    blk = R[si * n + 8 * tg:si * n + 8 * tg + 8, :]
                    acc[hp][sg + tg] = acc[hp][sg + tg] + blk * vrow
    o_intra = jnp.concatenate([jnp.concatenate(a, axis=0) for a in acc], axis=1)

    onw = on_ref[...]
    qtb = _pad_rows(qt, 16).astype(BF16)
    ktb = _pad_rows(kt, 16).astype(BF16)
    vb = _pad_rows(v, 16).astype(BF16)
    for h in range(HG_HEADS):
        sl = slice(h * HG_DK, (h + 1) * HG_DK)
        Sh = ST[h]
        o_h = lax.dot_general(qtb[:, sl], Sh.astype(BF16), (((1,), (1,)), ((), ())),
                              preferred_element_type=F32)[:C] + o_intra[:, sl]
        on = o_h * lax.rsqrt(jnp.mean(o_h * o_h, axis=-1, keepdims=True) + EPS) * onw
        g = pg[:, sl]
        o_ref[:, sl] = on * (g * _sigmoid(g))
        upd = lax.dot_general(vb[:, sl], ktb[:, sl], (((0,), (0,)), ((), ())),
                              preferred_element_type=F32)
        ST[h] = Sh * eblast[:, sl] + upd

    @pl.when(c == pl.num_programs(1) - 1)
    def _():
        for h in range(HG_HEADS):
            sout_ref[h] = ST[h].T


def _hgrn(proj, lb, onorm, state, chunk, bb):
    B, T, _ = proj.shape
    width = HG_HEADS * HG_DK
    col = lambda j: pl.BlockSpec((bb, chunk, width), lambda b, c, j=j: (b, c, j))
    st = pl.BlockSpec((bb, HG_HEADS, HG_DK, HG_DV), lambda b, c: (b, 0, 0, 0))
    return pl.pallas_call(
        functools.partial(_hgrn_body, C=chunk, BB=bb),
        grid=(B // bb, T // chunk),
        in_specs=[col(COL_PQ // width), col(COL_PF // width), col(COL_PI // width), col(COL_PG // width),
                  pl.BlockSpec(lb.shape, lambda b, c: (0, 0)),
                  pl.BlockSpec((1, HG_DV), lambda b, c: (0, 0)),
                  st],
        out_specs=[pl.BlockSpec((bb, chunk, width), lambda b, c: (b, c, 0)), st],
        out_shape=[jax.ShapeDtypeStruct((B, T, width), F32),
                   jax.ShapeDtypeStruct(state.shape, F32)],
        scratch_shapes=[pltpu.VMEM((bb, HG_HEADS, HG_DV, HG_DK), F32)],
        compiler_params=_params("parallel", "arbitrary"),
        name="hgrn",
    )(proj, proj, proj, proj, lb, onorm, state)


def _rope_tables(pos):
    half = HEAD_DIM // 2
    inv = ROPE_THETA ** (-jnp.arange(half, dtype=F32) / half)
    ang = pos.astype(F32)[:, None] * inv[None, :]
    cos, sin = jnp.cos(ang), jnp.sin(ang)
    cos = jnp.concatenate([cos, cos, cos, cos], axis=1)
    sin = jnp.concatenate([-sin, sin, -sin, sin], axis=1)
    return cos, sin


def _rope(x, cos, sin):
    lane = lax.broadcasted_iota(jnp.int32, (x.shape[0], LANES), 1)
    first = (lane % HEAD_DIM) < (HEAD_DIM // 2)
    outs = []
    for s in range(x.shape[1] // LANES):
        xs = x[:, s * LANES:(s + 1) * LANES]
        partner = jnp.where(first, pltpu.roll(xs, LANES - HEAD_DIM // 2, axis=1),
                            pltpu.roll(xs, HEAD_DIM // 2, axis=1))
        outs.append(xs * cos + partner * sin)
    return jnp.concatenate(outs, axis=1)


def _sink_softmax_pv(s, sink, vg):
    m = jnp.maximum(jnp.max(s, axis=-1, keepdims=True), sink)
    p = jnp.exp(s - m)
    den = jnp.sum(p, axis=-1, keepdims=True) + jnp.exp(sink - m)
    return jnp.dot(p.astype(BF16), vg, preferred_element_type=F32) / den


def _swa_prompt_body(q_ref, k_ref, v_ref, cos_ref, sin_ref, sink_ref, o_ref, krot_ref, kprev, vprev,
                     *, first_valid):
    i = pl.program_id(0)

    @pl.when(i == 0)
    def _():
        kprev[...] = jnp.zeros_like(kprev)
        vprev[...] = jnp.zeros_like(vprev)

    cos, sin = cos_ref[...], sin_ref[...]
    qr = _rope(q_ref[...], cos, sin)
    kr = _rope(k_ref[...], cos, sin)
    v = v_ref[...]
    krot_ref[...] = kr
    kk = jnp.concatenate([kprev[...], kr], axis=0).astype(BF16)
    vv = jnp.concatenate([vprev[...], v], axis=0).astype(BF16)
    R = GROUP * WINDOW
    r = lax.broadcasted_iota(jnp.int32, (R, 2 * WINDOW), 0) % WINDOW
    j = lax.broadcasted_iota(jnp.int32, (R, 2 * WINDOW), 1)
    mask = (j > r) & (j <= r + WINDOW) & ((i - 1) * WINDOW + j >= first_valid)
    hrow = lax.broadcasted_iota(jnp.int32, (R, 1), 0) // WINDOW
    for g in range(ATT_KV_HEADS):
        kg = kk[:, g * HEAD_DIM:(g + 1) * HEAD_DIM]
        vg = vv[:, g * HEAD_DIM:(g + 1) * HEAD_DIM]
        qg = jnp.concatenate([qr[:, (g * GROUP + hh) * HEAD_DIM:(g * GROUP + hh + 1) * HEAD_DIM]
                              for hh in range(GROUP)], axis=0).astype(BF16)
        sink = jnp.zeros((R, 1), F32)
        for hh in range(GROUP):
            sink = jnp.where(hrow == hh, sink_ref[g * GROUP + hh], sink)
        s = lax.dot_general(qg, kg, (((1,), (1,)), ((), ())),
                            preferred_element_type=F32) * (HEAD_DIM ** -0.5)
        o = _sink_softmax_pv(jnp.where(mask, s, -jnp.inf), sink, vg)
        for hh in range(GROUP):
            h = g * GROUP + hh
            o_ref[:, h * HEAD_DIM:(h + 1) * HEAD_DIM] = o[hh * WINDOW:(hh + 1) * WINDOW]
    kprev[...] = kr
    vprev[...] = v


def _swa_prompt(proj, cos, sin, sinks, first_valid):
    T = proj.shape[0]
    nb = T // WINDOW
    return pl.pallas_call(
        functools.partial(_swa_prompt_body, first_valid=first_valid),
        grid=(nb,),
        in_specs=[pl.BlockSpec((WINDOW, D_MODEL), lambda i: (i, COL_AQ // D_MODEL)),
                  pl.BlockSpec((WINDOW, KV_WIDTH), lambda i: (i, COL_AK // KV_WIDTH)),
                  pl.BlockSpec((WINDOW, KV_WIDTH), lambda i: (i, COL_AV // KV_WIDTH)),
                  pl.BlockSpec((WINDOW, LANES), lambda i: (i, 0)),
                  pl.BlockSpec((WINDOW, LANES), lambda i: (i, 0)),
                  pl.BlockSpec(memory_space=pltpu.SMEM)],
        out_specs=[pl.BlockSpec((WINDOW, D_MODEL), lambda i: (i, 0)),
                   pl.BlockSpec((WINDOW, KV_WIDTH), lambda i: (i, 0))],
        out_shape=[jax.ShapeDtypeStruct((T, D_MODEL), F32),
                   jax.ShapeDtypeStruct((T, KV_WIDTH), F32)],
        scratch_shapes=[pltpu.VMEM((WINDOW, KV_WIDTH), F32)] * 2,
        compiler_params=_params("arbitrary"),
        name="swa_prompt",
    )(proj, proj, proj, cos, sin, sinks)


def _swa_sample_body(q_ref, k_ref, v_ref, ck_ref, cv_ref, cos_ref, sin_ref, sink_ref, o_ref, ko_ref, vo_ref,
                     *, SB, L, WC):
    cos, sin = cos_ref[...], sin_ref[...]
    qr = _rope(q_ref[...], cos, sin)
    kr = _rope(k_ref[...], cos, sin)
    v = v_ref[...]
    for b in range(SB):
        ko_ref[b, 0:WC - L, :] = ck_ref[b, L:WC, :]
        ko_ref[b, WC - L:WC, :] = kr[b * L:(b + 1) * L, :]
        vo_ref[b, 0:WC - L, :] = cv_ref[b, L:WC, :]
        vo_ref[b, WC - L:WC, :] = v[b * L:(b + 1) * L, :]
    NB = SB * L
    R = GROUP * NB
    NC = SB * WC
    rr = lax.broadcasted_iota(jnp.int32, (R, NC + NB), 0)
    j = lax.broadcasted_iota(jnp.int32, (R, NC + NB), 1)
    qseq, qtok = (rr % NB) // L, rr % L
    cached = j < NC
    kseq = jnp.where(cached, j // WC, (j - NC) // L)
    rel = jnp.where(cached, qtok + WC - j % WC, qtok - (j - NC) % L)
    mask = (kseq == qseq) & (rel >= 0) & (rel < WINDOW)
    hrow = lax.broadcasted_iota(jnp.int32, (R, 1), 0) // NB
    for g in range(ATT_KV_HEADS):
        cols = slice(g * HEAD_DIM, (g + 1) * HEAD_DIM)
        keys = jnp.concatenate([ck_ref[b][:, cols] for b in range(SB)] + [kr[:, cols]], axis=0).astype(BF16)
        vals = jnp.concatenate([cv_ref[b][:, cols] for b in range(SB)] + [v[:, cols]], axis=0).astype(BF16)
        qg = jnp.concatenate([qr[:, (g * GROUP + hh) * HEAD_DIM:(g * GROUP + hh + 1) * HEAD_DIM]
                              for hh in range(GROUP)], axis=0).astype(BF16)
        sink = jnp.zeros((R, 1), F32)
        for hh in range(GROUP):
            sink = jnp.where(hrow == hh, sink_ref[g * GROUP + hh], sink)
        s = lax.dot_general(qg, keys, (((1,), (1,)), ((), ())),
                            preferred_element_type=F32) * (HEAD_DIM ** -0.5)
        o = _sink_softmax_pv(jnp.where(mask, s, -jnp.inf), sink, vals)
        for hh in range(GROUP):
            h = g * GROUP + hh
            o_ref[:, h * HEAD_DIM:(h + 1) * HEAD_DIM] = o[hh * NB:(hh + 1) * NB]


def _swa_sample(proj, cache_k, cache_v, cos, sin, sinks, L, SB):
    rows = proj.shape[0]
    Bd, WC, _ = cache_k.shape
    blk = SB * L
    return pl.pallas_call(
        functools.partial(_swa_sample_body, SB=SB, L=L, WC=WC),
        grid=(Bd // SB,),
        in_specs=[pl.BlockSpec((blk, D_MODEL), lambda i: (i, COL_AQ // D_MODEL)),
                  pl.BlockSpec((blk, KV_WIDTH), lambda i: (i, COL_AK // KV_WIDTH)),
                  pl.BlockSpec((blk, KV_WIDTH), lambda i: (i, COL_AV // KV_WIDTH)),
                  pl.BlockSpec((SB, WC, KV_WIDTH), lambda i: (i, 0, 0)),
                  pl.BlockSpec((SB, WC, KV_WIDTH), lambda i: (i, 0, 0)),
                  pl.BlockSpec((blk, LANES), lambda i: (0, 0)),
                  pl.BlockSpec((blk, LANES), lambda i: (0, 0)),
                  pl.BlockSpec(memory_space=pltpu.SMEM)],
        out_specs=[pl.BlockSpec((blk, D_MODEL), lambda i: (i, 0)),
                   pl.BlockSpec((SB, WC, KV_WIDTH), lambda i: (i, 0, 0)),
                   pl.BlockSpec((SB, WC, KV_WIDTH), lambda i: (i, 0, 0))],
        out_shape=[jax.ShapeDtypeStruct((rows, D_MODEL), F32),
                   jax.ShapeDtypeStruct(cache_k.shape, F32),
                   jax.ShapeDtypeStruct(cache_v.shape, F32)],
        compiler_params=_params("parallel"),
        name="swa_sample",
    )(proj, proj, proj, cache_k, cache_v, cos, sin, sinks)


def _merge_body(h_ref, oa_ref, ob_ref, ga_ref, gb_ref, w_ref, o_ref):
    m = _sigmoid(ga_ref[...]) * oa_ref[...] + _sigmoid(gb_ref[...]) * ob_ref[...]
    o_ref[...] = h_ref[...] + jnp.dot(m.astype(BF16), w_ref[...], preferred_element_type=F32)


def _merge(h, o_a, o_b, proj, w_out_b, tm):
    rows = h.shape[0]
    row = lambda j: pl.BlockSpec((tm, D_MODEL), lambda i, j=j: (i, j))
    return pl.pallas_call(
        _merge_body,
        grid=(rows // tm,),
        in_specs=[row(0), row(0), row(0), row(COL_GA // D_MODEL), row(COL_GB // D_MODEL),
                  pl.BlockSpec((D_MODEL, D_MODEL), lambda i: (0, 0))],
        out_specs=row(0),
        out_shape=jax.ShapeDtypeStruct((rows, D_MODEL), F32),
        compiler_params=_params("parallel"),
        name="merge",
    )(h, o_a, o_b, proj, proj, w_out_b)


def _top_values(x, n, ranked=False):
    vals = []
    rank = jnp.full(x.shape, float(n), F32)
    for it in range(n):
        m = jnp.max(x, axis=0, keepdims=True)
        vals.append(m)
        hit = x == m
        x = jnp.where(hit, -jnp.inf, x)
        if ranked:
            rank = jnp.where(hit, float(it), rank)
    top = jnp.concatenate(vals, axis=0)
    return (top, rank) if ranked else top


def _staircase_sums(t1, t2):
    K = PEER_TOPK
    sub = lax.broadcasted_iota(jnp.int32, (8, t1.shape[1]), 0)
    pieces = [t1[0:1, :] + t2, t1[1:2, :] + t2[0:8, :]]
    for i in range(2, 8):
        pieces.append(jnp.where(sub < K // (i + 1), t1[i:i + 1, :] + t2[0:8, :], -jnp.inf))
    pieces.append(t1[8:16, :] + t2[0:1, :])
    return jnp.concatenate(pieces, axis=0)


def _rows_bf16(row, rows):
    tile = jnp.broadcast_to(row, (16, row.shape[1])).astype(BF16)
    return jnp.concatenate([tile] * (rows // 16), axis=0)


def _peer_body(h_ref, nf_ref, nfin_ref, wq_ref, sub_ref, u0_ref, un_ref, vt_ref, o_ref,
               xnT, m1T, e1T, r2T, e2T, act_s, yT, *, TB, EC):
    e = pl.program_id(1)
    K = PEER_TOPK
    slot = e % 2

    @pl.when(e == 0)
    def _():
        xt = _rms(h_ref[...], nf_ref[...]).T.astype(BF16)
        xnT[...] = xt
        yT[...] = jnp.zeros(yT.shape, F32)
        act_s[0] = jnp.dot(u0_ref[...], xt, preferred_element_type=F32)
        for h in range(PEER_HEADS):
            scores = []
            for p in range(2):
                hp = 2 * h + p
                qhp = jnp.dot(wq_ref[hp * LANES:(hp + 1) * LANES, :], xt, preferred_element_type=F32)
                scores.append(jnp.dot(sub_ref[hp], qhp.astype(BF16), preferred_element_type=F32))
            s1, s2 = scores
            t1 = _top_values(s1, K)
            t2, rank2 = _top_values(s2, K, ranked=True)
            c = _top_values(_staircase_sums(t1, t2), K)
            tau = c[K - 1:K, :]
            z = jnp.sum(jnp.exp(c - c[0:1, :]), axis=0, keepdims=True)
            m = jnp.zeros_like(s1)
            for j in range(4):
                m = jnp.where(s1 + t2[j:j + 1, :] >= tau, float(j + 1), m)
            for r in range(3):
                cnt = jnp.sum(jnp.where(t1[r:r + 1, :] + t2 >= tau, 1.0, 0.0), axis=0, keepdims=True)
                m = jnp.where(s1 == t1[r:r + 1, :], cnt, m)
            m1T[h] = m
            r2T[h] = rank2.astype(BF16)
            e1T[h] = jnp.exp(s1 - t1[0:1, :]) / z
            e2T[h] = jnp.exp(s2 - t2[0:1, :]).astype(BF16)

    act_next = jnp.dot(un_ref[...], xnT[...], preferred_element_type=F32)
    SLAB = 2 * PEER_NKEYS
    contrib = None
    for j in range(EC // SLAB):
        parts = []
        for al in range(2 * j, 2 * j + 2):
            a = e * (EC // PEER_NKEYS) + al
            w = jnp.zeros((PEER_NKEYS, TB), BF16)
            for h in range(PEER_HEADS):
                sel = r2T[h] < _rows_bf16(m1T[h, pl.ds(a, 1), :], PEER_NKEYS)
                gate = _rows_bf16(e1T[h, pl.ds(a, 1), :], PEER_NKEYS) * e2T[h]
                w = w + jnp.where(sel, gate, jnp.zeros_like(gate))
            x = act_s[slot, al * PEER_NKEYS:(al + 1) * PEER_NKEYS, :]
            gelu = 0.5 * x * (1.0 + lax.erf(x * (2.0 ** -0.5)))
            parts.append(w * gelu.astype(BF16))
        g = jnp.concatenate(parts, axis=0)
        d = jnp.dot(vt_ref[:, j * SLAB:(j + 1) * SLAB], g, preferred_element_type=F32)
        contrib = d if contrib is None else contrib + d
    act_s[1 - slot] = act_next
    yT[...] += contrib

    @pl.when(e == pl.num_programs(1) - 1)
    def _():
        h3 = h_ref[...] + yT[...].T
        o_ref[...] = _rms(h3, nfin_ref[...])


def _peer(h2, row0, rows, nf, nfin, wqT_b, sub_b, u_b, vT_b, TB, EC):
    nexp = u_b.shape[0]
    ne = nexp // EC
    off = row0 // TB
    f = pl.pallas_call(
        functools.partial(_peer_body, TB=TB, EC=EC),
        grid=(rows // TB, ne),
        in_specs=[pl.BlockSpec((TB, D_MODEL), lambda i, e: (i + off, 0)),
                  pl.BlockSpec((1, D_MODEL), lambda i, e: (0, 0)),
                  pl.BlockSpec((1, D_MODEL), lambda i, e: (0, 0)),
                  pl.BlockSpec(wqT_b.shape, lambda i, e: (0, 0)),
                  pl.BlockSpec(sub_b.shape, lambda i, e: (0, 0, 0)),
                  pl.BlockSpec((EC, D_MODEL), lambda i, e: (0, 0)),
                  pl.BlockSpec((EC, D_MODEL), lambda i, e: (jnp.minimum(e + 1, ne - 1), 0)),
                  pl.BlockSpec((D_MODEL, EC), lambda i, e: (0, e))],
        out_specs=pl.BlockSpec((TB, D_MODEL), lambda i, e: (i, 0)),
        out_shape=jax.ShapeDtypeStruct((rows, D_MODEL), F32),
        scratch_shapes=[pltpu.VMEM((D_MODEL, TB), BF16),
                        pltpu.VMEM((PEER_HEADS, PEER_NKEYS, TB), F32),
                        pltpu.VMEM((PEER_HEADS, PEER_NKEYS, TB), F32),
                        pltpu.VMEM((PEER_HEADS, PEER_NKEYS, TB), BF16),
                        pltpu.VMEM((PEER_HEADS, PEER_NKEYS, TB), BF16),
                        pltpu.VMEM((2, EC, TB), F32),
                        pltpu.VMEM((D_MODEL, TB), F32)],
        compiler_params=_params("parallel", "arbitrary"),
        name="peer",
    )
    return f(h2, nf, nfin, wqT_b, sub_b, u_b, u_b, vT_b)


PEER_TB = 256
PEER_EC = 1024
PROJ_TN = 1280
HGRN_CHUNK = 32
SAMPLE_SB = 8
SAMPLE_HB = 4


def _row_tile(rows, cap):
    best = 8
    for t in range(8, cap + 1, 8):
        if rows % t == 0:
            best = t
    return best


def kernel(x_prompt, x_sample, state_hgrn, cache_swa_k, cache_swa_v, meta_tokens, w_in, hgrn_lb,
           hgrn_onorm, attn_sinks, w_out, norm_mix, norm_ffn, peer_query, peer_subkeys, peer_u, peer_v,
           norm_final):
    assert w_in.shape[0] == 1, "single-layer trunk"
    B, SEQ, _ = x_prompt.shape
    Bd, L, _ = x_sample.shape
    assert B == 1
    past = PAST_LEN
    wc = cache_swa_k.shape[2]
    START = PEER_TB
    T = START + SEQ
    first_valid = START - N_META

    sizes = np.cumsum([0, 1024, 1024, 1024, 1024, 1024, 256, 256, 1024, 1024])
    seg = lambda i: w_in[0][:, sizes[i]:sizes[i + 1]]
    w_in_b = jnp.concatenate([seg(0), seg(1), seg(2), seg(3), seg(4), seg(7), seg(8), seg(5), seg(6)],
                             axis=1).astype(BF16)
    w_out_b = w_out[0].astype(BF16)
    wqT_b = peer_query[0].T.astype(BF16)
    sub_b = peer_subkeys[0].reshape(PEER_HEADS * 2, PEER_NKEYS, -1).astype(BF16)
    u_b = peer_u[0].astype(BF16)
    vT_b = peer_v[0].T.astype(BF16)
    nmix = norm_mix[0].reshape(1, D_MODEL)
    nffn = norm_ffn[0].reshape(1, D_MODEL)
    nfin = norm_final.reshape(1, D_MODEL)
    onorm = hgrn_onorm[0].reshape(1, HG_DV)
    sinks = attn_sinks[0]

    hp = jnp.concatenate([jnp.zeros((first_valid, D_MODEL), F32), meta_tokens.astype(F32), x_prompt[0]],
                         axis=0)
    proj_p = _norm_proj(hp, nmix, w_in_b, _row_tile(T, 1280), PROJ_TN)
    oa_p, st_p = _hgrn(proj_p[None], hgrn_lb, onorm,
                       jnp.zeros((1, HG_HEADS, HG_DK, HG_DV), F32), HGRN_CHUNK, 1)
    cos_p, sin_p = _rope_tables(jnp.arange(T) - first_valid)
    ob_p, krot_p = _swa_prompt(proj_p, cos_p, sin_p, sinks, first_valid)
    h2_p = _merge(hp, oa_p[0], ob_p, proj_p, w_out_b, _row_tile(T, 512))
    y_prompt = _peer(h2_p, START, SEQ, nffn, nfin, wqT_b, sub_b, u_b, vT_b, PEER_TB, PEER_EC)[None]
    wp = min(WINDOW, SEQ + N_META)
    kp = krot_p[T - wp:].reshape(1, 1, wp, ATT_KV_HEADS, HEAD_DIM)
    vp = proj_p[T - wp:, COL_AV:COL_AV + KV_WIDTH].reshape(1, 1, wp, ATT_KV_HEADS, HEAD_DIM)

    hs = x_sample.reshape(Bd * L, D_MODEL)
    proj_s = _norm_proj(hs, nmix, w_in_b, _row_tile(Bd * L, 512), PROJ_TN)
    oa_s, st_s = _hgrn(proj_s.reshape(Bd, L, PROJ_COLS), hgrn_lb, onorm, state_hgrn[0], L, SAMPLE_HB)
    cos_s, sin_s = _rope_tables(past + jnp.arange(L))
    cos_s, sin_s = jnp.tile(cos_s, (SAMPLE_SB, 1)), jnp.tile(sin_s, (SAMPLE_SB, 1))
    ck = cache_swa_k[0].reshape(Bd, wc, KV_WIDTH)
    cv = cache_swa_v[0].reshape(Bd, wc, KV_WIDTH)
    ob_s, ks, vs = _swa_sample(proj_s, ck, cv, cos_s, sin_s, sinks, L, SAMPLE_SB)
    h2_s = _merge(hs, oa_s.reshape(Bd * L, D_MODEL), ob_s, proj_s, w_out_b, _row_tile(Bd * L, 512))
    y_sample = _peer(h2_s, 0, Bd * L, nffn, nfin, wqT_b, sub_b, u_b, vT_b, PEER_TB, PEER_EC)
    y_sample = y_sample.reshape(Bd, L, D_MODEL)
    ks = ks.reshape(1, Bd, wc, ATT_KV_HEADS, HEAD_DIM)
    vs = vs.reshape(1, Bd, wc, ATT_KV_HEADS, HEAD_DIM)

    return (y_prompt, y_sample, st_p[None], st_s[None], kp, ks, vp, vs)
```

```python
import functools

import jax
import jax.numpy as jnp
import numpy as np
from jax import lax
from jax.experimental import pallas as pl
from jax.experimental.pallas import tpu as pltpu

F32 = jnp.float32
BF16 = jnp.bfloat16

D_MODEL = 1024
N_META = 16
HG_HEADS = 8
HG_DK = 128
HG_DV = 128
ATT_HEADS = 16
ATT_KV_HEADS = 4
GROUP = ATT_HEADS // ATT_KV_HEADS
HEAD_DIM = 64
KV_WIDTH = ATT_KV_HEADS * HEAD_DIM
WINDOW = 128
ROPE_THETA = 10000.0
PAST_LEN = 16384
PEER_HEADS = 8
PEER_NKEYS = 128
PEER_TOPK = 16
EPS = 1e-6
LOG2E = 1.4426950408889634

LANES = 128
VMEM_LIMIT = 56 * 1024 * 1024

COL_PQ, COL_PF, COL_PI, COL_PG, COL_AQ, COL_GA, COL_GB = (i * D_MODEL for i in range(7))
COL_AK = 7 * D_MODEL
COL_AV = COL_AK + KV_WIDTH
PROJ_COLS = COL_AV + KV_WIDTH


def _sigmoid(x):
    return jax.nn.sigmoid(x)


def _rms(x, w):
    return x * lax.rsqrt(jnp.mean(x * x, axis=-1, keepdims=True) + EPS) * w


def _params(*sem):
    return pltpu.CompilerParams(dimension_semantics=sem, vmem_limit_bytes=VMEM_LIMIT)


def _proj_body(x_ref, nw_ref, w_ref, o_ref, xn_ref):
    @pl.when(pl.program_id(1) == 0)
    def _():
        xn_ref[...] = _rms(x_ref[...], nw_ref[...]).astype(BF16)

    o_ref[...] = jnp.dot(xn_ref[...], w_ref[...], preferred_element_type=F32)


def _norm_proj(x, nw, w_b, tm, tn):
    rows, n = x.shape[0], w_b.shape[1]
    return pl.pallas_call(
        _proj_body,
        grid=(rows // tm, n // tn),
        in_specs=[pl.BlockSpec((tm, D_MODEL), lambda i, j: (i, 0)),
                  pl.BlockSpec((1, D_MODEL), lambda i, j: (0, 0)),
                  pl.BlockSpec((D_MODEL, tn), lambda i, j: (0, j))],
        out_specs=pl.BlockSpec((tm, tn), lambda i, j: (i, j)),
        out_shape=jax.ShapeDtypeStruct((rows, n), F32),
        scratch_shapes=[pltpu.VMEM((tm, D_MODEL), BF16)],
        compiler_params=_params("parallel", "arbitrary"),
        name="norm_proj",
    )(x, nw, w_b)


def _pad_rows(x, rows):
    if x.shape[0] >= rows:
        return x
    return jnp.concatenate([x, jnp.zeros((rows - x.shape[0], x.shape[1]), x.dtype)], axis=0)


def _hgrn_body(pq_ref, pf_ref, pi_ref, pg_ref, lb_ref, on_ref, s0_ref, o_ref, sout_ref, ST, *, C, BB):
    for bi in range(BB):
        _hgrn_one(pq_ref.at[bi], pf_ref.at[bi], pi_ref.at[bi], pg_ref.at[bi], lb_ref, on_ref,
                  s0_ref.at[bi], o_ref.at[bi], sout_ref.at[bi], ST.at[bi], C=C)


def _hgrn_one(pq_ref, pf_ref, pi_ref, pg_ref, lb_ref, on_ref, s0_ref, o_ref, sout_ref, ST, *, C):
    c = pl.program_id(1)
    W2 = 2 * HG_DK
    NG = C // 8

    @pl.when(c == 0)
    def _():
        for h in range(HG_HEADS):
            ST[h] = s0_ref[h].T

    lbr = lb_ref[...]
    lbe = jnp.exp(lbr - jnp.max(lbr, axis=0, keepdims=True))
    lb = lbe[0:1] / jnp.sum(lbe, axis=0, keepdims=True)

    pq = pq_ref[...]
    pg = pg_ref[...]
    q = pq * _sigmoid(pq)
    f = lb + (1.0 - lb) * _sigmoid(pf_ref[...])
    k = 1.0 - f
    v = pi_ref[...]

    row = lax.broadcasted_iota(jnp.int32, (C, HG_HEADS * HG_DK), 0)
    b = jnp.log(f)
    sh = 1
    while sh < C:
        b = b + jnp.where(row >= sh, pltpu.roll(b, sh, axis=0), 0.0)
        sh *= 2
    blast = b[C - 1:C, :]
    qt = q * jnp.exp(b)
    kt = k * jnp.exp(blast - b)
    eblast = jnp.exp(blast)

    r2 = lax.broadcasted_iota(jnp.int32, (W2, W2), 0) // HG_DK
    c2 = lax.broadcasted_iota(jnp.int32, (W2, W2), 1) // HG_DK
    ones_blk = (r2 == c2).astype(BF16)
    b2 = b * LOG2E
    cs = b2 - jnp.log2(k)
    sub8 = lax.broadcasted_iota(jnp.int32, (8, HG_HEADS * HG_DK), 0)
    acc = [[jnp.zeros((8, W2), F32) for _ in range(NG)] for _ in range(HG_HEADS // 2)]
    for sg in range(NG):
        t0 = 8 * sg
        n = C - t0
        bt, qq = b2[t0:, :], q[t0:, :]
        pieces = []
        for s in range(t0, t0 + 8):
            d = bt - cs[s:s + 1, :]
            d = jnp.concatenate([jnp.where(sub8 >= s - t0, d[:8], -jnp.inf), d[8:]], axis=0) if n > 8 \
                else jnp.where(sub8 >= s - t0, d, -jnp.inf)
            pieces.append(jnp.exp2(d) * qq)
        E = jnp.concatenate(pieces, axis=0).astype(BF16)
        for hp in range(HG_HEADS // 2):
            lanes = slice(hp * W2, (hp + 1) * W2)
            R = jnp.dot(E[:, lanes], ones_blk, preferred_element_type=F32)
            for si in range(8):
                vrow = v[t0 + si:t0 + si + 1, lanes]
                for tg in range(n // 8):
                    blk = R[si * n + 8 * tg:si * n + 8 * tg + 8, :]
                    acc[hp][sg + tg] = acc[hp][sg + tg] + blk * vrow
    o_intra = jnp.concatenate([jnp.concatenate(a, axis=0) for a in acc], axis=1)

    onw = on_ref[...]
    qtb = _pad_rows(qt, 16).astype(BF16)
    ktb = _pad_rows(kt, 16).astype(BF16)
    vb = _pad_rows(v, 16).astype(BF16)
    for h in range(HG_HEADS):
        sl = slice(h * HG_DK, (h + 1) * HG_DK)
        Sh = ST[h]
        o_h = lax.dot_general(qtb[:, sl], Sh.astype(BF16), (((1,), (1,)), ((), ())),
                              preferred_element_type=F32)[:C] + o_intra[:, sl]
        on = o_h * lax.rsqrt(jnp.mean(o_h * o_h, axis=-1, keepdims=True) + EPS) * onw
        g = pg[:, sl]
        o_ref[:, sl] = on * (g * _sigmoid(g))
        upd = lax.dot_general(vb[:, sl], ktb[:, sl], (((0,), (0,)), ((), ())),
                              preferred_element_type=F32)
        ST[h] = Sh * eblast[:, sl] + upd

    @pl.when(c == pl.num_programs(1) - 1)
    def _():
        for h in range(HG_HEADS):
            sout_ref[h] = ST[h].T


def _hgrn(proj, lb, onorm, state, chunk, bb):
    B, T, _ = proj.shape
    width = HG_HEADS * HG_DK
    col = lambda j: pl.BlockSpec((bb, chunk, width), lambda b, c, j=j: (b, c, j))
    st = pl.BlockSpec((bb, HG_HEADS, HG_DK, HG_DV), lambda b, c: (b, 0, 0, 0))
    return pl.pallas_call(
        functools.partial(_hgrn_body, C=chunk, BB=bb),
        grid=(B // bb, T // chunk),
        in_specs=[col(COL_PQ // width), col(COL_PF // width), col(COL_PI // width), col(COL_PG // width),
                  pl.BlockSpec(lb.shape, lambda b, c: (0, 0)),
                  pl.BlockSpec((1, HG_DV), lambda b, c: (0, 0)),
                  st],
        out_specs=[pl.BlockSpec((bb, chunk, width), lambda b, c: (b, c, 0)), st],
        out_shape=[jax.ShapeDtypeStruct((B, T, width), F32),
                   jax.ShapeDtypeStruct(state.shape, F32)],
        scratch_shapes=[pltpu.VMEM((bb, HG_HEADS, HG_DV, HG_DK), F32)],
        compiler_params=_params("parallel", "arbitrary"),
        name="hgrn",
    )(proj, proj, proj, proj, lb, onorm, state)


def _rope_tables(pos):
    half = HEAD_DIM // 2
    inv = ROPE_THETA ** (-jnp.arange(half, dtype=F32) / half)
    ang = pos.astype(F32)[:, None] * inv[None, :]
    cos, sin = jnp.cos(ang), jnp.sin(ang)
    cos = jnp.concatenate([cos, cos, cos, cos], axis=1)
    sin = jnp.concatenate([-sin, sin, -sin, sin], axis=1)
    return cos, sin


def _rope(x, cos, sin):
    lane = lax.broadcasted_iota(jnp.int32, (x.shape[0], LANES), 1)
    first = (lane % HEAD_DIM) < (HEAD_DIM // 2)
    outs = []
    for s in range(x.shape[1] // LANES):
        xs = x[:, s * LANES:(s + 1) * LANES]
        partner = jnp.where(first, pltpu.roll(xs, LANES - HEAD_DIM // 2, axis=1),
                            pltpu.roll(xs, HEAD_DIM // 2, axis=1))
        outs.append(xs * cos + partner * sin)
    return jnp.concatenate(outs, axis=1)


def _sink_softmax_pv(s, sink, vg):
    m = jnp.maximum(jnp.max(s, axis=-1, keepdims=True), sink)
    p = jnp.exp2(s - m)
    den = jnp.sum(p, axis=-1, keepdims=True) + jnp.exp2(sink - m)
    return jnp.dot(p.astype(BF16), vg, preferred_element_type=F32) / den


def _swa_prompt_body(q_ref, k_ref, v_ref, cos_ref, sin_ref, sink_ref, o_ref, krot_ref, kprev, vprev,
                     *, first_valid):
    i = pl.program_id(0)

    @pl.when(i == 0)
    def _():
        kprev[...] = jnp.zeros_like(kprev)
        vprev[...] = jnp.zeros_like(vprev)

    cos, sin = cos_ref[...], sin_ref[...]
    qr = _rope(q_ref[...], cos, sin)
    kr = _rope(k_ref[...], cos, sin)
    v = v_ref[...]
    krot_ref[...] = kr
    kk = jnp.concatenate([kprev[...], kr], axis=0).astype(BF16)
    vv = jnp.concatenate([vprev[...], v], axis=0).astype(BF16)
    R = GROUP * WINDOW
    r = lax.broadcasted_iota(jnp.int32, (R, 2 * WINDOW), 0) % WINDOW
    j = lax.broadcasted_iota(jnp.int32, (R, 2 * WINDOW), 1)
    mask = (j > r) & (j <= r + WINDOW) & ((i - 1) * WINDOW + j >= first_valid)
    hrow = lax.broadcasted_iota(jnp.int32, (R, 1), 0) // WINDOW
    for g in range(ATT_KV_HEADS):
        kg = kk[:, g * HEAD_DIM:(g + 1) * HEAD_DIM]
        vg = vv[:, g * HEAD_DIM:(g + 1) * HEAD_DIM]
        qg = jnp.concatenate([qr[:, (g * GROUP + hh) * HEAD_DIM:(g * GROUP + hh + 1) * HEAD_DIM]
                              for hh in range(GROUP)], axis=0).astype(BF16)
        sink = jnp.zeros((R, 1), F32)
        for hh in range(GROUP):
            sink = jnp.where(hrow == hh, sink_ref[g * GROUP + hh] * LOG2E, sink)
        s = lax.dot_general(qg, kg, (((1,), (1,)), ((), ())),
                            preferred_element_type=F32) * (HEAD_DIM ** -0.5 * LOG2E)
        o = _sink_softmax_pv(jnp.where(mask, s, -jnp.inf), sink, vg)
        for hh in range(GROUP):
            h = g * GROUP + hh
            o_ref[:, h * HEAD_DIM:(h + 1) * HEAD_DIM] = o[hh * WINDOW:(hh + 1) * WINDOW]
    kprev[...] = kr
    vprev[...] = v


def _swa_prompt(proj, cos, sin, sinks, first_valid):
    T = proj.shape[0]
    nb = T // WINDOW
    return pl.pallas_call(
        functools.partial(_swa_prompt_body, first_valid=first_valid),
        grid=(nb,),
        in_specs=[pl.BlockSpec((WINDOW, D_MODEL), lambda i: (i, COL_AQ // D_MODEL)),
                  pl.BlockSpec((WINDOW, KV_WIDTH), lambda i: (i, COL_AK // KV_WIDTH)),
                  pl.BlockSpec((WINDOW, KV_WIDTH), lambda i: (i, COL_AV // KV_WIDTH)),
                  pl.BlockSpec((WINDOW, LANES), lambda i: (i, 0)),
                  pl.BlockSpec((WINDOW, LANES), lambda i: (i, 0)),
                  pl.BlockSpec(memory_space=pltpu.SMEM)],
        out_specs=[pl.BlockSpec((WINDOW, D_MODEL), lambda i: (i, 0)),
                   pl.BlockSpec((WINDOW, KV_WIDTH), lambda i: (i, 0))],
        out_shape=[jax.ShapeDtypeStruct((T, D_MODEL), F32),
                   jax.ShapeDtypeStruct((T, KV_WIDTH), F32)],
        scratch_shapes=[pltpu.VMEM((WINDOW, KV_WIDTH), F32)] * 2,
        compiler_params=_params("arbitrary"),
        name="swa_prompt",
    )(proj, proj, proj, cos, sin, sinks)


def _swa_sample_body(q_ref, k_ref, v_ref, ck_ref, cv_ref, cos_ref, sin_ref, sink_ref, o_ref, ko_ref, vo_ref,
                     *, SB, L, WC):
    cos, sin = cos_ref[...], sin_ref[...]
    qr = _rope(q_ref[...], cos, sin)
    kr = _rope(k_ref[...], cos, sin)
    v = v_ref[...]
    for b in range(SB):
        ko_ref[b, 0:WC - L, :] = ck_ref[b, L:WC, :]
        ko_ref[b, WC - L:WC, :] = kr[b * L:(b + 1) * L, :]
        vo_ref[b, 0:WC - L, :] = cv_ref[b, L:WC, :]
        vo_ref[b, WC - L:WC, :] = v[b * L:(b + 1) * L, :]
    NB = SB * L
    R = GROUP * NB
    NC = SB * WC
    rr = lax.broadcasted_iota(jnp.int32, (R, NC + NB), 0)
    j = lax.broadcasted_iota(jnp.int32, (R, NC + NB), 1)
    qseq, qtok = (rr % NB) // L, rr % L
    cached = j < NC
    kseq = jnp.where(cached, j // WC, (j - NC) // L)
    rel = jnp.where(cached, qtok + WC - j % WC, qtok - (j - NC) % L)
    mask = (kseq == qseq) & (rel >= 0) & (rel < WINDOW)
    hrow = lax.broadcasted_iota(jnp.int32, (R, 1), 0) // NB
    for g in range(ATT_KV_HEADS):
        cols = slice(g * HEAD_DIM, (g + 1) * HEAD_DIM)
        keys = jnp.concatenate([ck_ref[b][:, cols] for b in range(SB)] + [kr[:, cols]], axis=0).astype(BF16)
        vals = jnp.concatenate([cv_ref[b][:, cols] for b in range(SB)] + [v[:, cols]], axis=0).astype(BF16)
        qg = jnp.concatenate([qr[:, (g * GROUP + hh) * HEAD_DIM:(g * GROUP + hh + 1) * HEAD_DIM]
                              for hh in range(GROUP)], axis=0).astype(BF16)
        sink = jnp.zeros((R, 1), F32)
        for hh in range(GROUP):
            sink = jnp.where(hrow == hh, sink_ref[g * GROUP + hh] * LOG2E, sink)
        s = lax.dot_general(qg, keys, (((1,), (1,)), ((), ())),
                            preferred_element_type=F32) * (HEAD_DIM ** -0.5 * LOG2E)
        o = _sink_softmax_pv(jnp.where(mask, s, -jnp.inf), sink, vals)
        for hh in range(GROUP):
            h = g * GROUP + hh
            o_ref[:, h * HEAD_DIM:(h + 1) * HEAD_DIM] = o[hh * NB:(hh + 1) * NB]


def _swa_sample(proj, cache_k, cache_v, cos, sin, sinks, L, SB):
    rows = proj.shape[0]
    Bd, WC, _ = cache_k.shape
    blk = SB * L
    return pl.pallas_call(
        functools.partial(_swa_sample_body, SB=SB, L=L, WC=WC),
        grid=(Bd // SB,),
        in_specs=[pl.BlockSpec((blk, D_MODEL), lambda i: (i, COL_AQ // D_MODEL)),
                  pl.BlockSpec((blk, KV_WIDTH), lambda i: (i, COL_AK // KV_WIDTH)),
                  pl.BlockSpec((blk, KV_WIDTH), lambda i: (i, COL_AV // KV_WIDTH)),
                  pl.BlockSpec((SB, WC, KV_WIDTH), lambda i: (i, 0, 0)),
                  pl.BlockSpec((SB, WC, KV_WIDTH), lambda i: (i, 0, 0)),
                  pl.BlockSpec((blk, LANES), lambda i: (0, 0)),
                  pl.BlockSpec((blk, LANES), lambda i: (0, 0)),
                  pl.BlockSpec(memory_space=pltpu.SMEM)],
        out_specs=[pl.BlockSpec((blk, D_MODEL), lambda i: (i, 0)),
                   pl.BlockSpec((SB, WC, KV_WIDTH), lambda i: (i, 0, 0)),
                   pl.BlockSpec((SB, WC, KV_WIDTH), lambda i: (i, 0, 0))],
        out_shape=[jax.ShapeDtypeStruct((rows, D_MODEL), F32),
                   jax.ShapeDtypeStruct(cache_k.shape, F32),
                   jax.ShapeDtypeStruct(cache_v.shape, F32)],
        compiler_params=_params("parallel"),
        name="swa_sample",
    )(proj, proj, proj, cache_k, cache_v, cos, sin, sinks)


def _merge_body(h_ref, oa_ref, ob_ref, ga_ref, gb_ref, w_ref, o_ref):
    m = _sigmoid(ga_ref[...]) * oa_ref[...] + _sigmoid(gb_ref[...]) * ob_ref[...]
    o_ref[...] = h_ref[...] + jnp.dot(m.astype(BF16), w_ref[...], preferred_element_type=F32)


def _merge(h, o_a, o_b, proj, w_out_b, tm):
    rows = h.shape[0]
    row = lambda j: pl.BlockSpec((tm, D_MODEL), lambda i, j=j: (i, j))
    return pl.pallas_call(
        _merge_body,
        grid=(rows // tm,),
        in_specs=[row(0), row(0), row(0), row(COL_GA // D_MODEL), row(COL_GB // D_MODEL),
                  pl.BlockSpec((D_MODEL, D_MODEL), lambda i: (0, 0))],
        out_specs=row(0),
        out_shape=jax.ShapeDtypeStruct((rows, D_MODEL), F32),
        compiler_params=_params("parallel"),
        name="merge",
    )(h, o_a, o_b, proj, proj, w_out_b)


def _top_values(x, n, ranked=False):
    vals = []
    rank = jnp.full(x.shape, float(n), F32)
    for it in range(n):
        m = jnp.max(x, axis=0, keepdims=True)
        vals.append(m)
        hit = x == m
        x = jnp.where(hit, -jnp.inf, x)
        if ranked:
            rank = jnp.where(hit, float(it), rank)
    top = jnp.concatenate(vals, axis=0)
    return (top, rank) if ranked else top


def _staircase_sums(t1, t2):
    K = PEER_TOPK
    sub = lax.broadcasted_iota(jnp.int32, (8, t1.shape[1]), 0)
    pieces = [t1[0:1, :] + t2, t1[1:2, :] + t2[0:8, :]]
    for i in range(2, 8):
        pieces.append(jnp.where(sub < K // (i + 1), t1[i:i + 1, :] + t2[0:8, :], -jnp.inf))
    pieces.append(t1[8:16, :] + t2[0:1, :])
    return jnp.concatenate(pieces, axis=0)


def _rows_bf16(row, rows):
    tile = jnp.broadcast_to(row, (16, row.shape[1])).astype(BF16)
    return jnp.concatenate([tile] * (rows // 16), axis=0)


RING = 3


def _peer_body(h_ref, nf_ref, nfin_ref, wq_ref, sub_ref, u0_ref, u_hbm, vt_hbm, o_ref,
               xnT, m1T, e1T, r2T, e2T, act_s, yT, ubuf, vbuf, sem, *, TB, EC, NB):
    i = pl.program_id(0)
    e = pl.program_id(1)
    ne = pl.num_programs(1)
    K = PEER_TOPK
    slot = e % 2
    SLABS = EC // (2 * PEER_NKEYS)
    step = i * ne + e

    def table_copies(e_of_step, ring_slot):
        u_chunk = (e_of_step + 1) % ne
        return (pltpu.make_async_copy(u_hbm.at[pl.ds(pl.multiple_of(u_chunk * EC, EC), EC), :],
                                      ubuf.at[ring_slot], sem.at[0, ring_slot]),
                pltpu.make_async_copy(vt_hbm.at[pl.ds(pl.multiple_of(e_of_step * SLABS, SLABS), SLABS)],
                                      vbuf.at[ring_slot], sem.at[1, ring_slot]))

    @pl.when(step == 0)
    def _():
        for d in range(RING - 1):
            for cp in table_copies(d, d):
                cp.start()

    @pl.when(step + RING - 1 < NB * ne)
    def _():
        for cp in table_copies((e + RING - 1) % ne, (step + RING - 1) % RING):
            cp.start()

    ring_slot = step % RING
    for cp in table_copies(e, ring_slot):
        cp.wait()
    un_ref = ubuf.at[ring_slot]
    vt_ref = vbuf.at[ring_slot]

    @pl.when(e == 0)
    def _():
        xt = _rms(h_ref[...], nf_ref[...]).T.astype(BF16)
        xnT[...] = xt
        yT[...] = jnp.zeros(yT.shape, F32)
        act_s[0] = jnp.dot(u0_ref[...], xt, preferred_element_type=F32)
        for h in range(PEER_HEADS):
            scores = []
            for p in range(2):
                hp = 2 * h + p
                qhp = jnp.dot(wq_ref[hp * LANES:(hp + 1) * LANES, :], xt, preferred_element_type=F32)
                scores.append(jnp.dot(sub_ref[hp], qhp.astype(BF16), preferred_element_type=F32))
            s1, s2 = scores
            t1 = _top_values(s1, K)
            t2, rank2 = _top_values(s2, K, ranked=True)
            c = _top_values(_staircase_sums(t1, t2), K)
            tau = c[K - 1:K, :]
            z = jnp.sum(jnp.exp(c - c[0:1, :]), axis=0, keepdims=True)
            m = jnp.zeros_like(s1)
            for j in range(4):
                m = jnp.where(s1 + t2[j:j + 1, :] >= tau, float(j + 1), m)
            for r in range(3):
                cnt = jnp.sum(jnp.where(t1[r:r + 1, :] + t2 >= tau, 1.0, 0.0), axis=0, keepdims=True)
                m = jnp.where(s1 == t1[r:r + 1, :], cnt, m)
            m1T[h] = m
            r2T[h] = rank2.astype(BF16)
            e1T[h] = jnp.exp(s1 - t1[0:1, :]) / z
            e2T[h] = jnp.exp(s2 - t2[0:1, :]).astype(BF16)

    act_next = jnp.dot(un_ref[...], xnT[...], preferred_element_type=F32)
    SLAB = 2 * PEER_NKEYS
    contrib = None
    for j in range(EC // SLAB):
        parts = []
        for al in range(2 * j, 2 * j + 2):
            a = e * (EC // PEER_NKEYS) + al
            w = jnp.zeros((PEER_NKEYS, TB), BF16)
            for h in range(PEER_HEADS):
                sel = r2T[h] < _rows_bf16(m1T[h, pl.ds(a, 1), :], PEER_NKEYS)
                gate = _rows_bf16(e1T[h, pl.ds(a, 1), :], PEER_NKEYS) * e2T[h]
                w = w + jnp.where(sel, gate, jnp.zeros_like(gate))
            x = act_s[slot, al * PEER_NKEYS:(al + 1) * PEER_NKEYS, :]
            gelu = 0.5 * x * (1.0 + lax.erf(x * (2.0 ** -0.5)))
            parts.append(w * gelu.astype(BF16))
        g = jnp.concatenate(parts, axis=0)
        d = jnp.dot(vt_ref[j], g, preferred_element_type=F32)
        contrib = d if contrib is None else contrib + d
    act_s[1 - slot] = act_next
    yT[...] += contrib

    @pl.when(e == pl.num_programs(1) - 1)
    def _():
        h3 = h_ref[...] + yT[...].T
        o_ref[...] = _rms(h3, nfin_ref[...])


def _peer(h2, row0, rows, nf, nfin, wqT_b, sub_b, u_b, vT_b, TB, EC):
    nexp = u_b.shape[0]
    ne = nexp // EC
    nb = rows // TB
    off = row0 // TB
    slab = 2 * PEER_NKEYS
    assert RING - 1 < ne and vT_b.shape == (nexp // slab, D_MODEL, slab)
    f = pl.pallas_call(
        functools.partial(_peer_body, TB=TB, EC=EC, NB=nb),
        grid=(nb, ne),
        in_specs=[pl.BlockSpec((TB, D_MODEL), lambda i, e: (i + off, 0)),
                  pl.BlockSpec((1, D_MODEL), lambda i, e: (0, 0)),
                  pl.BlockSpec((1, D_MODEL), lambda i, e: (0, 0)),
                  pl.BlockSpec(wqT_b.shape, lambda i, e: (0, 0)),
                  pl.BlockSpec(sub_b.shape, lambda i, e: (0, 0, 0)),
                  pl.BlockSpec((EC, D_MODEL), lambda i, e: (0, 0)),
                  pl.BlockSpec(memory_space=pl.ANY),
                  pl.BlockSpec(memory_space=pl.ANY)],
        out_specs=pl.BlockSpec((TB, D_MODEL), lambda i, e: (i, 0)),
        out_shape=jax.ShapeDtypeStruct((rows, D_MODEL), F32),
        scratch_shapes=[pltpu.VMEM((D_MODEL, TB), BF16),
                        pltpu.VMEM((PEER_HEADS, PEER_NKEYS, TB), F32),
                        pltpu.VMEM((PEER_HEADS, PEER_NKEYS, TB), F32),
                        pltpu.VMEM((PEER_HEADS, PEER_NKEYS, TB), BF16),
                        pltpu.VMEM((PEER_HEADS, PEER_NKEYS, TB), BF16),
                        pltpu.VMEM((2, EC, TB), F32),
                        pltpu.VMEM((D_MODEL, TB), F32),
                        pltpu.VMEM((RING, EC, D_MODEL), BF16),
                        pltpu.VMEM((RING, EC // slab, D_MODEL, slab), BF16),
                        pltpu.SemaphoreType.DMA((2, RING))],
        compiler_params=_params("arbitrary", "arbitrary"),
        name="peer",
    )
    return f(h2, nf, nfin, wqT_b, sub_b, u_b, u_b, vT_b)


PEER_TB = 256
PEER_EC = 1024
PROJ_TN = 1280
HGRN_CHUNK = 32
SAMPLE_SB = 8
SAMPLE_HB = 4


def _row_tile(rows, cap):
    best = 8
    for t in range(8, cap + 1, 8):
        if rows % t == 0:
            best = t
    return best


def kernel(x_prompt, x_sample, state_hgrn, cache_swa_k, cache_swa_v, meta_tokens, w_in, hgrn_lb,
           hgrn_onorm, attn_sinks, w_out, norm_mix, norm_ffn, peer_query, peer_subkeys, peer_u, peer_v,
           norm_final):
    assert w_in.shape[0] == 1, "single-layer trunk"
    B, SEQ, _ = x_prompt.shape
    Bd, L, _ = x_sample.shape
    assert B == 1
    past = PAST_LEN
    wc = cache_swa_k.shape[2]
    START = PEER_TB
    T = START + SEQ
    first_valid = START - N_META

    sizes = np.cumsum([0, 1024, 1024, 1024, 1024, 1024, 256, 256, 1024, 1024])
    seg = lambda i: w_in[0][:, sizes[i]:sizes[i + 1]]
    w_in_b = jnp.concatenate([seg(0), seg(1), seg(2), seg(3), seg(4), seg(7), seg(8), seg(5), seg(6)],
                             axis=1).astype(BF16)
    w_out_b = w_out[0].astype(BF16)
    wqT_b = peer_query[0].T.astype(BF16)
    sub_b = peer_subkeys[0].reshape(PEER_HEADS * 2, PEER_NKEYS, -1).astype(BF16)
    u_b = peer_u[0].astype(BF16)
    vT_b = peer_v[0].astype(BF16).reshape(-1, 2 * PEER_NKEYS, D_MODEL).transpose(0, 2, 1)
    nmix = norm_mix[0].reshape(1, D_MODEL)
    nffn = norm_ffn[0].reshape(1, D_MODEL)
    nfin = norm_final.reshape(1, D_MODEL)
    onorm = hgrn_onorm[0].reshape(1, HG_DV)
    sinks = attn_sinks[0]

    hp = jnp.concatenate([jnp.zeros((first_valid, D_MODEL), F32), meta_tokens.astype(F32), x_prompt[0]],
                         axis=0)
    proj_p = _norm_proj(hp, nmix, w_in_b, _row_tile(T, 1280), PROJ_TN)
    oa_p, st_p = _hgrn(proj_p[None], hgrn_lb, onorm,
                       jnp.zeros((1, HG_HEADS, HG_DK, HG_DV), F32), HGRN_CHUNK, 1)
    cos_p, sin_p = _rope_tables(jnp.arange(T) - first_valid)
    ob_p, krot_p = _swa_prompt(proj_p, cos_p, sin_p, sinks, first_valid)
    h2_p = _merge(hp, oa_p[0], ob_p, proj_p, w_out_b, _row_tile(T, 512))
    y_prompt = _peer(h2_p, START, SEQ, nffn, nfin, wqT_b, sub_b, u_b, vT_b, PEER_TB, PEER_EC)[None]
    wp = min(WINDOW, SEQ + N_META)
    kp = krot_p[T - wp:].reshape(1, 1, wp, ATT_KV_HEADS, HEAD_DIM)
    vp = proj_p[T - wp:, COL_AV:COL_AV + KV_WIDTH].reshape(1, 1, wp, ATT_KV_HEADS, HEAD_DIM)

    hs = x_sample.reshape(Bd * L, D_MODEL)
    proj_s = _norm_proj(hs, nmix, w_in_b, _row_tile(Bd * L, 512), PROJ_TN)
    oa_s, st_s = _hgrn(proj_s.reshape(Bd, L, PROJ_COLS), hgrn_lb, onorm, state_hgrn[0], L, SAMPLE_HB)
    cos_s, sin_s = _rope_tables(past + jnp.arange(L))
    cos_s, sin_s = jnp.tile(cos_s, (SAMPLE_SB, 1)), jnp.tile(sin_s, (SAMPLE_SB, 1))
    ck = cache_swa_k[0].reshape(Bd, wc, KV_WIDTH)
    cv = cache_swa_v[0].reshape(Bd, wc, KV_WIDTH)
    ob_s, ks, vs = _swa_sample(proj_s, ck, cv, cos_s, sin_s, sinks, L, SAMPLE_SB)
    h2_s = _merge(hs, oa_s.reshape(Bd * L, D_MODEL), ob_s, proj_s, w_out_b, _row_tile(Bd * L, 512))
    y_sample = _peer(h2_s, 0, Bd * L, nffn, nfin, wqT_b, sub_b, u_b, vT_b, PEER_TB, PEER_EC)
    y_sample = y_sample.reshape(Bd, L, D_MODEL)
    ks = ks.reshape(1, Bd, wc, ATT_KV_HEADS, HEAD_DIM)
    vs = vs.reshape(1, Bd, wc, ATT_KV_HEADS, HEAD_DIM)

    return (y_prompt, y_sample, st_p[None], st_s[None], kp, ks, vp, vs)
```

```python
import functools

import jax
import jax.numpy as jnp
import numpy as np
from jax import lax
from jax.experimental import pallas as pl
from jax.experimental.pallas import tpu as pltpu

F32 = jnp.float32
BF16 = jnp.bfloat16

D_MODEL = 1024
N_META = 16
HG_HEADS = 8
HG_DK = 128
HG_DV = 128
ATT_HEADS = 16
ATT_KV_HEADS = 4
GROUP = ATT_HEADS // ATT_KV_HEADS
HEAD_DIM = 64
KV_WIDTH = ATT_KV_HEADS * HEAD_DIM
WINDOW = 128
ROPE_THETA = 10000.0
PAST_LEN = 16384
PEER_HEADS = 8
PEER_NKEYS = 128
PEER_TOPK = 16
EPS = 1e-6
LOG2E = 1.4426950408889634

LANES = 128
VMEM_LIMIT = 56 * 1024 * 1024

COL_PQ, COL_PF, COL_PI, COL_PG, COL_AQ, COL_GA, COL_GB = (i * D_MODEL for i in range(7))
COL_AK = 7 * D_MODEL
COL_AV = COL_AK + KV_WIDTH
PROJ_COLS = COL_AV + KV_WIDTH


def _sigmoid(x):
    return jax.nn.sigmoid(x)


def _rms(x, w):
    return x * lax.rsqrt(jnp.mean(x * x, axis=-1, keepdims=True) + EPS) * w


def _params(*sem):
    return pltpu.CompilerParams(dimension_semantics=sem, vmem_limit_bytes=VMEM_LIMIT)


def _proj_body(x_ref, nw_ref, w_ref, o_ref, xn_ref):
    @pl.when(pl.program_id(1) == 0)
    def _():
        xn_ref[...] = _rms(x_ref[...], nw_ref[...]).astype(BF16)

    o_ref[...] = jnp.dot(xn_ref[...], w_ref[...], preferred_element_type=F32)


def _norm_proj(x, nw, w_b, tm, tn):
    rows, n = x.shape[0], w_b.shape[1]
    return pl.pallas_call(
        _proj_body,
        grid=(rows // tm, n // tn),
        in_specs=[pl.BlockSpec((tm, D_MODEL), lambda i, j: (i, 0)),
                  pl.BlockSpec((1, D_MODEL), lambda i, j: (0, 0)),
                  pl.BlockSpec((D_MODEL, tn), lambda i, j: (0, j))],
        out_specs=pl.BlockSpec((tm, tn), lambda i, j: (i, j)),
        out_shape=jax.ShapeDtypeStruct((rows, n), F32),
        scratch_shapes=[pltpu.VMEM((tm, D_MODEL), BF16)],
        compiler_params=_params("parallel", "arbitrary"),
        name="norm_proj",
    )(x, nw, w_b)


def _pad_rows(x, rows):
    if x.shape[0] >= rows:
        return x
    return jnp.concatenate([x, jnp.zeros((rows - x.shape[0], x.shape[1]), x.dtype)], axis=0)


def _hgrn_body(pq_ref, pf_ref, pi_ref, pg_ref, lb_ref, on_ref, s0_ref, o_ref, sout_ref, ST, *, C, BB, NCH):
    c = pl.program_id(1)

    @pl.when(c == 0)
    def _():
        for bi in range(BB):
            for h in range(HG_HEADS):
                ST[bi, h] = s0_ref[bi, h].T

    lbr = lb_ref[...]
    lbe = jnp.exp(lbr - jnp.max(lbr, axis=0, keepdims=True))
    lb = lbe[0:1] / jnp.sum(lbe, axis=0, keepdims=True)

    for ci in range(NCH):
        rows = pl.ds(ci * C, C)
        for bi in range(BB):
            _hgrn_chunk(pq_ref.at[bi, rows], pf_ref.at[bi, rows], pi_ref.at[bi, rows], pg_ref.at[bi, rows],
                        lb, on_ref, o_ref.at[bi, rows], ST.at[bi], C=C)

    @pl.when(c == pl.num_programs(1) - 1)
    def _():
        for bi in range(BB):
            for h in range(HG_HEADS):
                sout_ref[bi, h] = ST[bi, h].T


def _hgrn_chunk(pq_ref, pf_ref, pi_ref, pg_ref, lb, on_ref, o_ref, ST, *, C):
    W2 = 2 * HG_DK
    NG = C // 8

    pq = pq_ref[...]
    pg = pg_ref[...]
    q = pq * _sigmoid(pq)
    f = lb + (1.0 - lb) * _sigmoid(pf_ref[...])
    k = 1.0 - f
    v = pi_ref[...]

    row = lax.broadcasted_iota(jnp.int32, (C, HG_HEADS * HG_DK), 0)
    b = jnp.log(f)
    sh = 1
    while sh < C:
        b = b + jnp.where(row >= sh, pltpu.roll(b, sh, axis=0), 0.0)
        sh *= 2
    blast = b[C - 1:C, :]
    qt = q * jnp.exp(b)
    kt = k * jnp.exp(blast - b)
    eblast = jnp.exp(blast)

    r2 = lax.broadcasted_iota(jnp.int32, (W2, W2), 0) // HG_DK
    c2 = lax.broadcasted_iota(jnp.int32, (W2, W2), 1) // HG_DK
    ones_blk = (r2 == c2).astype(BF16)
    b2 = b * LOG2E
    cs = b2 - jnp.log2(k)
    sub8 = lax.broadcasted_iota(jnp.int32, (8, HG_HEADS * HG_DK), 0)
    acc = [[jnp.zeros((8, W2), F32) for _ in range(NG)] for _ in range(HG_HEADS // 2)]
    for sg in range(NG):
        t0 = 8 * sg
        n = C - t0
        bt, qq = b2[t0:, :], q[t0:, :]
        pieces = []
        for s in range(t0, t0 + 8):
            d = bt - cs[s:s + 1, :]
            d = jnp.concatenate([jnp.where(sub8 >= s - t0, d[:8], -jnp.inf), d[8:]], axis=0) if n > 8 \
                else jnp.where(sub8 >= s - t0, d, -jnp.inf)
            pieces.append(jnp.exp2(d) * qq)
        E = jnp.concatenate(pieces, axis=0).astype(BF16)
        for hp in range(HG_HEADS // 2):
            lanes = slice(hp * W2, (hp + 1) * W2)
            R = jnp.dot(E[:, lanes], ones_blk, preferred_element_type=F32)
            for si in range(8):
                vrow = v[t0 + si:t0 + si + 1, lanes]
                for tg in range(n // 8):
                    blk = R[si * n + 8 * tg:si * n + 8 * tg + 8, :]
                    acc[hp][sg + tg] = acc[hp][sg + tg] + blk * vrow
    o_intra = jnp.concatenate([jnp.concatenate(a, axis=0) for a in acc], axis=1)

    onw = on_ref[...]
    qtb = _pad_rows(qt, 16).astype(BF16)
    ktb = _pad_rows(kt, 16).astype(BF16)
    vb = _pad_rows(v, 16).astype(BF16)
    for h in range(HG_HEADS):
        sl = slice(h * HG_DK, (h + 1) * HG_DK)
        Sh = ST[h]
        o_h = lax.dot_general(qtb[:, sl], Sh.astype(BF16), (((1,), (1,)), ((), ())),
                              preferred_element_type=F32)[:C] + o_intra[:, sl]
        on = o_h * lax.rsqrt(jnp.mean(o_h * o_h, axis=-1, keepdims=True) + EPS) * onw
        g = pg[:, sl]
        o_ref[:, sl] = on * (g * _sigmoid(g))
        upd = lax.dot_general(vb[:, sl], ktb[:, sl], (((0,), (0,)), ((), ())),
                              preferred_element_type=F32)
        ST[h] = Sh * eblast[:, sl] + upd


def _hgrn(proj, lb, onorm, state, chunk, bb, nch):
    B, T, _ = proj.shape
    width = HG_HEADS * HG_DK
    blk = chunk * nch
    col = lambda j: pl.BlockSpec((bb, blk, width), lambda b, c, j=j: (b, c, j))
    st = pl.BlockSpec((bb, HG_HEADS, HG_DK, HG_DV), lambda b, c: (b, 0, 0, 0))
    return pl.pallas_call(
        functools.partial(_hgrn_body, C=chunk, BB=bb, NCH=nch),
        grid=(B // bb, T // blk),
        in_specs=[col(COL_PQ // width), col(COL_PF // width), col(COL_PI // width), col(COL_PG // width),
                  pl.BlockSpec(lb.shape, lambda b, c: (0, 0)),
                  pl.BlockSpec((1, HG_DV), lambda b, c: (0, 0)),
                  st],
        out_specs=[pl.BlockSpec((bb, blk, width), lambda b, c: (b, c, 0)), st],
        out_shape=[jax.ShapeDtypeStruct((B, T, width), F32),
                   jax.ShapeDtypeStruct(state.shape, F32)],
        scratch_shapes=[pltpu.VMEM((bb, HG_HEADS, HG_DV, HG_DK), F32)],
        compiler_params=_params("parallel", "arbitrary"),
        name="hgrn",
    )(proj, proj, proj, proj, lb, onorm, state)


def _rope_tables(pos):
    half = HEAD_DIM // 2
    inv = ROPE_THETA ** (-jnp.arange(half, dtype=F32) / half)
    ang = pos.astype(F32)[:, None] * inv[None, :]
    cos, sin = jnp.cos(ang), jnp.sin(ang)
    cos = jnp.concatenate([cos, cos, cos, cos], axis=1)
    sin = jnp.concatenate([-sin, sin, -sin, sin], axis=1)
    return cos, sin


def _rope(x, cos, sin):
    lane = lax.broadcasted_iota(jnp.int32, (x.shape[0], LANES), 1)
    first = (lane % HEAD_DIM) < (HEAD_DIM // 2)
    outs = []
    for s in range(x.shape[1] // LANES):
        xs = x[:, s * LANES:(s + 1) * LANES]
        partner = jnp.where(first, pltpu.roll(xs, LANES - HEAD_DIM // 2, axis=1),
                            pltpu.roll(xs, HEAD_DIM // 2, axis=1))
        outs.append(xs * cos + partner * sin)
    return jnp.concatenate(outs, axis=1)


def _sink_softmax_pv(s, sink, vg):
    m = jnp.maximum(jnp.max(s, axis=-1, keepdims=True), sink)
    p = jnp.exp2(s - m)
    den = jnp.sum(p, axis=-1, keepdims=True) + jnp.exp2(sink - m)
    return jnp.dot(p.astype(BF16), vg, preferred_element_type=F32) / den


def _swa_prompt_body(q_ref, k_ref, v_ref, cos_ref, sin_ref, sink_ref, o_ref, krot_ref, kprev, vprev,
                     *, first_valid, NBK):
    i = pl.program_id(0)

    @pl.when(i == 0)
    def _():
        kprev[...] = jnp.zeros_like(kprev)
        vprev[...] = jnp.zeros_like(vprev)

    R = GROUP * WINDOW
    r = lax.broadcasted_iota(jnp.int32, (R, 2 * WINDOW), 0) % WINDOW
    j = lax.broadcasted_iota(jnp.int32, (R, 2 * WINDOW), 1)
    band = (j > r) & (j <= r + WINDOW)
    hrow = lax.broadcasted_iota(jnp.int32, (R, 1), 0) // WINDOW
    kp, vp = kprev[...], vprev[...]
    for bi in range(NBK):
        rows = slice(bi * WINDOW, (bi + 1) * WINDOW)
        cos, sin = cos_ref[rows, :], sin_ref[rows, :]
        qr = _rope(q_ref[rows, :], cos, sin)
        kr = _rope(k_ref[rows, :], cos, sin)
        v = v_ref[rows, :]
        krot_ref[rows, :] = kr
        kk = jnp.concatenate([kp, kr], axis=0).astype(BF16)
        vv = jnp.concatenate([vp, v], axis=0).astype(BF16)
        mask = band & ((i * NBK + bi - 1) * WINDOW + j >= first_valid)
        for g in range(ATT_KV_HEADS):
            kg = kk[:, g * HEAD_DIM:(g + 1) * HEAD_DIM]
            vg = vv[:, g * HEAD_DIM:(g + 1) * HEAD_DIM]
            qg = jnp.concatenate([qr[:, (g * GROUP + hh) * HEAD_DIM:(g * GROUP + hh + 1) * HEAD_DIM]
                                  for hh in range(GROUP)], axis=0).astype(BF16)
            sink = jnp.zeros((R, 1), F32)
            for hh in range(GROUP):
                sink = jnp.where(hrow == hh, sink_ref[g * GROUP + hh] * LOG2E, sink)
            s = lax.dot_general(qg, kg, (((1,), (1,)), ((), ())),
                                preferred_element_type=F32) * (HEAD_DIM ** -0.5 * LOG2E)
            o = _sink_softmax_pv(jnp.where(mask, s, -jnp.inf), sink, vg)
            for hh in range(GROUP):
                h = g * GROUP + hh
                o_ref[rows, h * HEAD_DIM:(h + 1) * HEAD_DIM] = o[hh * WINDOW:(hh + 1) * WINDOW]
        kp, vp = kr, v
    kprev[...] = kp
    vprev[...] = vp


def _swa_prompt(proj, cos, sin, sinks, first_valid, nbk):
    T = proj.shape[0]
    blk = nbk * WINDOW
    nb = T // blk
    return pl.pallas_call(
        functools.partial(_swa_prompt_body, first_valid=first_valid, NBK=nbk),
        grid=(nb,),
        in_specs=[pl.BlockSpec((blk, D_MODEL), lambda i: (i, COL_AQ // D_MODEL)),
                  pl.BlockSpec((blk, KV_WIDTH), lambda i: (i, COL_AK // KV_WIDTH)),
                  pl.BlockSpec((blk, KV_WIDTH), lambda i: (i, COL_AV // KV_WIDTH)),
                  pl.BlockSpec((blk, LANES), lambda i: (i, 0)),
                  pl.BlockSpec((blk, LANES), lambda i: (i, 0)),
                  pl.BlockSpec(memory_space=pltpu.SMEM)],
        out_specs=[pl.BlockSpec((blk, D_MODEL), lambda i: (i, 0)),
                   pl.BlockSpec((blk, KV_WIDTH), lambda i: (i, 0))],
        out_shape=[jax.ShapeDtypeStruct((T, D_MODEL), F32),
                   jax.ShapeDtypeStruct((T, KV_WIDTH), F32)],
        scratch_shapes=[pltpu.VMEM((WINDOW, KV_WIDTH), F32)] * 2,
        compiler_params=_params("arbitrary"),
        name="swa_prompt",
    )(proj, proj, proj, cos, sin, sinks)


def _swa_sample_body(q_ref, k_ref, v_ref, ck_ref, cv_ref, cos_ref, sin_ref, sink_ref, o_ref, ko_ref, vo_ref,
                     *, SB, L, WC):
    cos, sin = cos_ref[...], sin_ref[...]
    qr = _rope(q_ref[...], cos, sin)
    kr = _rope(k_ref[...], cos, sin)
    v = v_ref[...]
    for b in range(SB):
        ko_ref[b, 0:WC - L, :] = ck_ref[b, L:WC, :]
        ko_ref[b, WC - L:WC, :] = kr[b * L:(b + 1) * L, :]
        vo_ref[b, 0:WC - L, :] = cv_ref[b, L:WC, :]
        vo_ref[b, WC - L:WC, :] = v[b * L:(b + 1) * L, :]
    NB = SB * L
    R = GROUP * NB
    NC = SB * WC
    rr = lax.broadcasted_iota(jnp.int32, (R, NC + NB), 0)
    j = lax.broadcasted_iota(jnp.int32, (R, NC + NB), 1)
    qseq, qtok = (rr % NB) // L, rr % L
    cached = j < NC
    kseq = jnp.where(cached, j // WC, (j - NC) // L)
    rel = jnp.where(cached, qtok + WC - j % WC, qtok - (j - NC) % L)
    mask = (kseq == qseq) & (rel >= 0) & (rel < WINDOW)
    hrow = lax.broadcasted_iota(jnp.int32, (R, 1), 0) // NB
    for g in range(ATT_KV_HEADS):
        cols = slice(g * HEAD_DIM, (g + 1) * HEAD_DIM)
        keys = jnp.concatenate([ck_ref[b][:, cols] for b in range(SB)] + [kr[:, cols]], axis=0).astype(BF16)
        vals = jnp.concatenate([cv_ref[b][:, cols] for b in range(SB)] + [v[:, cols]], axis=0).astype(BF16)
        qg = jnp.concatenate([qr[:, (g * GROUP + hh) * HEAD_DIM:(g * GROUP + hh + 1) * HEAD_DIM]
                              for hh in range(GROUP)], axis=0).astype(BF16)
        sink = jnp.zeros((R, 1), F32)
        for hh in range(GROUP):
            sink = jnp.where(hrow == hh, sink_ref[g * GROUP + hh] * LOG2E, sink)
        s = lax.dot_general(qg, keys, (((1,), (1,)), ((), ())),
                            preferred_element_type=F32) * (HEAD_DIM ** -0.5 * LOG2E)
        o = _sink_softmax_pv(jnp.where(mask, s, -jnp.inf), sink, vals)
        for hh in range(GROUP):
            h = g * GROUP + hh
            o_ref[:, h * HEAD_DIM:(h + 1) * HEAD_DIM] = o[hh * NB:(hh + 1) * NB]


def _swa_sample(proj, cache_k, cache_v, cos, sin, sinks, L, SB):
    rows = proj.shape[0]
    Bd, WC, _ = cache_k.shape
    blk = SB * L
    return pl.pallas_call(
        functools.partial(_swa_sample_body, SB=SB, L=L, WC=WC),
        grid=(Bd // SB,),
        in_specs=[pl.BlockSpec((blk, D_MODEL), lambda i: (i, COL_AQ // D_MODEL)),
                  pl.BlockSpec((blk, KV_WIDTH), lambda i: (i, COL_AK // KV_WIDTH)),
                  pl.BlockSpec((blk, KV_WIDTH), lambda i: (i, COL_AV // KV_WIDTH)),
                  pl.BlockSpec((SB, WC, KV_WIDTH), lambda i: (i, 0, 0)),
                  pl.BlockSpec((SB, WC, KV_WIDTH), lambda i: (i, 0, 0)),
                  pl.BlockSpec((blk, LANES), lambda i: (0, 0)),
                  pl.BlockSpec((blk, LANES), lambda i: (0, 0)),
                  pl.BlockSpec(memory_space=pltpu.SMEM)],
        out_specs=[pl.BlockSpec((blk, D_MODEL), lambda i: (i, 0)),
                   pl.BlockSpec((SB, WC, KV_WIDTH), lambda i: (i, 0, 0)),
                   pl.BlockSpec((SB, WC, KV_WIDTH), lambda i: (i, 0, 0))],
        out_shape=[jax.ShapeDtypeStruct((rows, D_MODEL), F32),
                   jax.ShapeDtypeStruct(cache_k.shape, F32),
                   jax.ShapeDtypeStruct(cache_v.shape, F32)],
        compiler_params=_params("parallel"),
        name="swa_sample",
    )(proj, proj, proj, cache_k, cache_v, cos, sin, sinks)


def _merge_body(h_ref, oa_ref, ob_ref, ga_ref, gb_ref, w_ref, o_ref):
    m = _sigmoid(ga_ref[...]) * oa_ref[...] + _sigmoid(gb_ref[...]) * ob_ref[...]
    o_ref[...] = h_ref[...] + jnp.dot(m.astype(BF16), w_ref[...], preferred_element_type=F32)


def _merge(h, o_a, o_b, proj, w_out_b, tm):
    rows = h.shape[0]
    row = lambda j: pl.BlockSpec((tm, D_MODEL), lambda i, j=j: (i, j))
    return pl.pallas_call(
        _merge_body,
        grid=(rows // tm,),
        in_specs=[row(0), row(0), row(0), row(COL_GA // D_MODEL), row(COL_GB // D_MODEL),
                  pl.BlockSpec((D_MODEL, D_MODEL), lambda i: (0, 0))],
        out_specs=row(0),
        out_shape=jax.ShapeDtypeStruct((rows, D_MODEL), F32),
        compiler_params=_params("parallel"),
        name="merge",
    )(h, o_a, o_b, proj, proj, w_out_b)


def _top_values(x, n, ranked=False):
    vals = []
    rank = jnp.full(x.shape, float(n), F32)
    for it in range(n):
        m = jnp.max(x, axis=0, keepdims=True)
        vals.append(m)
        hit = x == m
        x = jnp.where(hit, -jnp.inf, x)
        if ranked:
            rank = jnp.where(hit, float(it), rank)
    top = jnp.concatenate(vals, axis=0)
    return (top, rank) if ranked else top


def _staircase_sums(t1, t2):
    K = PEER_TOPK
    sub = lax.broadcasted_iota(jnp.int32, (8, t1.shape[1]), 0)
    pieces = [t1[0:1, :] + t2, t1[1:2, :] + t2[0:8, :]]
    for i in range(2, 8):
        pieces.append(jnp.where(sub < K // (i + 1), t1[i:i + 1, :] + t2[0:8, :], -jnp.inf))
    pieces.append(t1[8:16, :] + t2[0:1, :])
    return jnp.concatenate(pieces, axis=0)


def _rows_bf16(row, rows):
    tile = jnp.broadcast_to(row, (16, row.shape[1])).astype(BF16)
    return jnp.concatenate([tile] * (rows // 16), axis=0)


RING = 3


def _peer_body(h_ref, nf_ref, nfin_ref, wq_ref, sub_ref, u0_ref, u_hbm, vt_hbm, o_ref,
               xnT, m1T, e1T, r2T, e2T, act_s, yT, ubuf, vbuf, sem, *, TB, EC, NB):
    i = pl.program_id(0)
    e = pl.program_id(1)
    ne = pl.num_programs(1)
    K = PEER_TOPK
    slot = e % 2
    SLABS = EC // (2 * PEER_NKEYS)
    step = i * ne + e

    def table_copies(e_of_step, ring_slot):
        u_chunk = (e_of_step + 1) % ne
        return (pltpu.make_async_copy(u_hbm.at[pl.ds(pl.multiple_of(u_chunk * EC, EC), EC), :],
                                      ubuf.at[ring_slot], sem.at[0, ring_slot]),
                pltpu.make_async_copy(vt_hbm.at[pl.ds(pl.multiple_of(e_of_step * SLABS, SLABS), SLABS)],
                                      vbuf.at[ring_slot], sem.at[1, ring_slot]))

    @pl.when(step == 0)
    def _():
        for d in range(RING - 1):
            for cp in table_copies(d, d):
                cp.start()

    @pl.when(step + RING - 1 < NB * ne)
    def _():
        for cp in table_copies((e + RING - 1) % ne, (step + RING - 1) % RING):
            cp.start()

    ring_slot = step % RING
    for cp in table_copies(e, ring_slot):
        cp.wait()
    un_ref = ubuf.at[ring_slot]
    vt_ref = vbuf.at[ring_slot]

    @pl.when(e == 0)
    def _():
        xt = _rms(h_ref[...], nf_ref[...]).T.astype(BF16)
        xnT[...] = xt
        yT[...] = jnp.zeros(yT.shape, F32)
        act_s[0] = jnp.dot(u0_ref[...], xt, preferred_element_type=F32)
        for h in range(PEER_HEADS):
            scores = []
            for p in range(2):
                hp = 2 * h + p
                qhp = jnp.dot(wq_ref[hp * LANES:(hp + 1) * LANES, :], xt, preferred_element_type=F32)
                scores.append(jnp.dot(sub_ref[hp], qhp.astype(BF16), preferred_element_type=F32))
            s1, s2 = scores
            t1 = _top_values(s1, K)
            t2, rank2 = _top_values(s2, K, ranked=True)
            c = _top_values(_staircase_sums(t1, t2), K)
            tau = c[K - 1:K, :]
            z = jnp.sum(jnp.exp(c - c[0:1, :]), axis=0, keepdims=True)
            m = jnp.zeros_like(s1)
            for j in range(4):
                m = jnp.where(s1 + t2[j:j + 1, :] >= tau, float(j + 1), m)
            for r in range(3):
                cnt = jnp.sum(jnp.where(t1[r:r + 1, :] + t2 >= tau, 1.0, 0.0), axis=0, keepdims=True)
                m = jnp.where(s1 == t1[r:r + 1, :], cnt, m)
            m1T[h] = m
            r2T[h] = rank2.astype(BF16)
            e1T[h] = jnp.exp(s1 - t1[0:1, :]) / z
            e2T[h] = jnp.exp(s2 - t2[0:1, :]).astype(BF16)

    act_next = jnp.dot(un_ref[...], xnT[...], preferred_element_type=F32)
    SLAB = 2 * PEER_NKEYS
    contrib = None
    for j in range(EC // SLAB):
        parts = []
        for al in range(2 * j, 2 * j + 2):
            a = e * (EC // PEER_NKEYS) + al
            w = jnp.zeros((PEER_NKEYS, TB), BF16)
            for h in range(PEER_HEADS):
                sel = r2T[h] < _rows_bf16(m1T[h, pl.ds(a, 1), :], PEER_NKEYS)
                gate = _rows_bf16(e1T[h, pl.ds(a, 1), :], PEER_NKEYS) * e2T[h]
                w = w + jnp.where(sel, gate, jnp.zeros_like(gate))
            x = act_s[slot, al * PEER_NKEYS:(al + 1) * PEER_NKEYS, :]
            gelu = 0.5 * x * (1.0 + lax.erf(x * (2.0 ** -0.5)))
            parts.append(w * gelu.astype(BF16))
        g = jnp.concatenate(parts, axis=0)
        d = jnp.dot(vt_ref[j], g, preferred_element_type=F32)
        contrib = d if contrib is None else contrib + d
    act_s[1 - slot] = act_next
    yT[...] += contrib

    @pl.when(e == pl.num_programs(1) - 1)
    def _():
        h3 = h_ref[...] + yT[...].T
        o_ref[...] = _rms(h3, nfin_ref[...])


def _peer(h2, row0, rows, nf, nfin, wqT_b, sub_b, u_b, vT_b, TB, EC):
    nexp = u_b.shape[0]
    ne = nexp // EC
    nb = rows // TB
    off = row0 // TB
    slab = 2 * PEER_NKEYS
    assert RING - 1 < ne and vT_b.shape == (nexp // slab, D_MODEL, slab)
    f = pl.pallas_call(
        functools.partial(_peer_body, TB=TB, EC=EC, NB=nb),
        grid=(nb, ne),
        in_specs=[pl.BlockSpec((TB, D_MODEL), lambda i, e: (i + off, 0)),
                  pl.BlockSpec((1, D_MODEL), lambda i, e: (0, 0)),
                  pl.BlockSpec((1, D_MODEL), lambda i, e: (0, 0)),
                  pl.BlockSpec(wqT_b.shape, lambda i, e: (0, 0)),
                  pl.BlockSpec(sub_b.shape, lambda i, e: (0, 0, 0)),
                  pl.BlockSpec((EC, D_MODEL), lambda i, e: (0, 0)),
                  pl.BlockSpec(memory_space=pl.ANY),
                  pl.BlockSpec(memory_space=pl.ANY)],
        out_specs=pl.BlockSpec((TB, D_MODEL), lambda i, e: (i, 0)),
        out_shape=jax.ShapeDtypeStruct((rows, D_MODEL), F32),
        scratch_shapes=[pltpu.VMEM((D_MODEL, TB), BF16),
                        pltpu.VMEM((PEER_HEADS, PEER_NKEYS, TB), F32),
                        pltpu.VMEM((PEER_HEADS, PEER_NKEYS, TB), F32),
                        pltpu.VMEM((PEER_HEADS, PEER_NKEYS, TB), BF16),
                        pltpu.VMEM((PEER_HEADS, PEER_NKEYS, TB), BF16),
                        pltpu.VMEM((2, EC, TB), F32),
                        pltpu.VMEM((D_MODEL, TB), F32),
                        pltpu.VMEM((RING, EC, D_MODEL), BF16),
                        pltpu.VMEM((RING, EC // slab, D_MODEL, slab), BF16),
                        pltpu.SemaphoreType.DMA((2, RING))],
        compiler_params=_params("arbitrary", "arbitrary"),
        name="peer",
    )
    return f(h2, nf, nfin, wqT_b, sub_b, u_b, u_b, vT_b)


PEER_TB = 256
PEER_EC = 1024
PROJ_TN = 1280
HGRN_CHUNK = 32
HGRN_NCH = 13
SWA_NBK = 1
SAMPLE_SB = 8
SAMPLE_HB = 4


def _row_tile(rows, cap):
    best = 8
    for t in range(8, cap + 1, 8):
        if rows % t == 0:
            best = t
    return best


def kernel(x_prompt, x_sample, state_hgrn, cache_swa_k, cache_swa_v, meta_tokens, w_in, hgrn_lb,
           hgrn_onorm, attn_sinks, w_out, norm_mix, norm_ffn, peer_query, peer_subkeys, peer_u, peer_v,
           norm_final):
    assert w_in.shape[0] == 1, "single-layer trunk"
    B, SEQ, _ = x_prompt.shape
    Bd, L, _ = x_sample.shape
    assert B == 1
    past = PAST_LEN
    wc = cache_swa_k.shape[2]
    START = PEER_TB
    T = START + SEQ
    first_valid = START - N_META

    sizes = np.cumsum([0, 1024, 1024, 1024, 1024, 1024, 256, 256, 1024, 1024])
    seg = lambda i: w_in[0][:, sizes[i]:sizes[i + 1]]
    w_in_b = jnp.concatenate([seg(0), seg(1), seg(2), seg(3), seg(4), seg(7), seg(8), seg(5), seg(6)],
                             axis=1).astype(BF16)
    w_out_b = w_out[0].astype(BF16)
    wqT_b = peer_query[0].T.astype(BF16)
    sub_b = peer_subkeys[0].reshape(PEER_HEADS * 2, PEER_NKEYS, -1).astype(BF16)
    u_b = peer_u[0].astype(BF16)
    vT_b = peer_v[0].astype(BF16).reshape(-1, 2 * PEER_NKEYS, D_MODEL).transpose(0, 2, 1)
    nmix = norm_mix[0].reshape(1, D_MODEL)
    nffn = norm_ffn[0].reshape(1, D_MODEL)
    nfin = norm_final.reshape(1, D_MODEL)
    onorm = hgrn_onorm[0].reshape(1, HG_DV)
    sinks = attn_sinks[0]

    hp = jnp.concatenate([jnp.zeros((first_valid, D_MODEL), F32), meta_tokens.astype(F32), x_prompt[0]],
                         axis=0)
    proj_p = _norm_proj(hp, nmix, w_in_b, _row_tile(T, 1280), PROJ_TN)
    oa_p, st_p = _hgrn(proj_p[None], hgrn_lb, onorm,
                       jnp.zeros((1, HG_HEADS, HG_DK, HG_DV), F32), HGRN_CHUNK, 1, HGRN_NCH)
    cos_p, sin_p = _rope_tables(jnp.arange(T) - first_valid)
    ob_p, krot_p = _swa_prompt(proj_p, cos_p, sin_p, sinks, first_valid, SWA_NBK)
    h2_p = _merge(hp, oa_p[0], ob_p, proj_p, w_out_b, _row_tile(T, 512))
    y_prompt = _peer(h2_p, START, SEQ, nffn, nfin, wqT_b, sub_b, u_b, vT_b, PEER_TB, PEER_EC)[None]
    wp = min(WINDOW, SEQ + N_META)
    kp = krot_p[T - wp:].reshape(1, 1, wp, ATT_KV_HEADS, HEAD_DIM)
    vp = proj_p[T - wp:, COL_AV:COL_AV + KV_WIDTH].reshape(1, 1, wp, ATT_KV_HEADS, HEAD_DIM)

    hs = x_sample.reshape(Bd * L, D_MODEL)
    proj_s = _norm_proj(hs, nmix, w_in_b, _row_tile(Bd * L, 512), PROJ_TN)
    oa_s, st_s = _hgrn(proj_s.reshape(Bd, L, PROJ_COLS), hgrn_lb, onorm, state_hgrn[0], L, SAMPLE_HB, 1)
    cos_s, sin_s = _rope_tables(past + jnp.arange(L))
    cos_s, sin_s = jnp.tile(cos_s, (SAMPLE_SB, 1)), jnp.tile(sin_s, (SAMPLE_SB, 1))
    ck = cache_swa_k[0].reshape(Bd, wc, KV_WIDTH)
    cv = cache_swa_v[0].reshape(Bd, wc, KV_WIDTH)
    ob_s, ks, vs = _swa_sample(proj_s, ck, cv, cos_s, sin_s, sinks, L, SAMPLE_SB)
    h2_s = _merge(hs, oa_s.reshape(Bd * L, D_MODEL), ob_s, proj_s, w_out_b, _row_tile(Bd * L, 512))
    y_sample = _peer(h2_s, 0, Bd * L, nffn, nfin, wqT_b, sub_b, u_b, vT_b, PEER_TB, PEER_EC)
    y_sample = y_sample.reshape(Bd, L, D_MODEL)
    ks = ks.reshape(1, Bd, wc, ATT_KV_HEADS, HEAD_DIM)
    vs = vs.reshape(1, Bd, wc, ATT_KV_HEADS, HEAD_DIM)

    return (y_prompt, y_sample, st_p[None], st_s[None], kp, ks, vp, vs)
```

```python
import functools

import jax
import jax.numpy as jnp
import numpy as np
from jax import lax
from jax.experimental import pallas as pl
from jax.experimental.pallas import tpu as pltpu

F32 = jnp.float32
BF16 = jnp.bfloat16

D_MODEL = 1024
N_META = 16
HG_HEADS = 8
HG_DK = 128
HG_DV = 128
ATT_HEADS = 16
ATT_KV_HEADS = 4
GROUP = ATT_HEADS // ATT_KV_HEADS
HEAD_DIM = 64
KV_WIDTH = ATT_KV_HEADS * HEAD_DIM
WINDOW = 128
ROPE_THETA = 10000.0
PAST_LEN = 16384
PEER_HEADS = 8
PEER_NKEYS = 128
PEER_TOPK = 16
EPS = 1e-6
LOG2E = 1.4426950408889634

LANES = 128
VMEM_LIMIT = 56 * 1024 * 1024

COL_PQ, COL_PF, COL_PI, COL_PG, COL_AQ, COL_GA, COL_GB = (i * D_MODEL for i in range(7))
COL_AK = 7 * D_MODEL
COL_AV = COL_AK + KV_WIDTH
PROJ_COLS = COL_AV + KV_WIDTH


def _sigmoid(x):
    return jax.nn.sigmoid(x)


def _rms(x, w):
    return x * lax.rsqrt(jnp.mean(x * x, axis=-1, keepdims=True) + EPS) * w


def _params(*sem):
    return pltpu.CompilerParams(dimension_semantics=sem, vmem_limit_bytes=VMEM_LIMIT)


def _proj_body(x_ref, nw_ref, w_ref, o_ref, xn_ref):
    @pl.when(pl.program_id(1) == 0)
    def _():
        xn_ref[...] = _rms(x_ref[...], nw_ref[...]).astype(BF16)

    o_ref[...] = jnp.dot(xn_ref[...], w_ref[...], preferred_element_type=F32)


def _norm_proj(x, nw, w_b, tm, tn):
    rows, n = x.shape[0], w_b.shape[1]
    return pl.pallas_call(
        _proj_body,
        grid=(rows // tm, n // tn),
        in_specs=[pl.BlockSpec((tm, D_MODEL), lambda i, j: (i, 0)),
                  pl.BlockSpec((1, D_MODEL), lambda i, j: (0, 0)),
                  pl.BlockSpec((D_MODEL, tn), lambda i, j: (0, j))],
        out_specs=pl.BlockSpec((tm, tn), lambda i, j: (i, j)),
        out_shape=jax.ShapeDtypeStruct((rows, n), F32),
        scratch_shapes=[pltpu.VMEM((tm, D_MODEL), BF16)],
        compiler_params=_params("parallel", "arbitrary"),
        name="norm_proj",
    )(x, nw, w_b)


def _pad_rows(x, rows):
    if x.shape[0] >= rows:
        return x
    return jnp.concatenate([x, jnp.zeros((rows - x.shape[0], x.shape[1]), x.dtype)], axis=0)


def _hgrn_body(pq_ref, pf_ref, pi_ref, pg_ref, lb_ref, on_ref, s0_ref, o_ref, sout_ref, ST, *, C, BB, NCH):
    c = pl.program_id(1)

    @pl.when(c == 0)
    def _():
        for bi in range(BB):
            for h in range(HG_HEADS):
                ST[bi, h] = s0_ref[bi, h].T

    lbr = lb_ref[...]
    lbe = jnp.exp(lbr - jnp.max(lbr, axis=0, keepdims=True))
    lb = lbe[0:1] / jnp.sum(lbe, axis=0, keepdims=True)

    for ci in range(NCH):
        rows = pl.ds(ci * C, C)
        for bi in range(BB):
            _hgrn_chunk(pq_ref.at[bi, rows], pf_ref.at[bi, rows], pi_ref.at[bi, rows], pg_ref.at[bi, rows],
                        lb, on_ref, o_ref.at[bi, rows], ST.at[bi], C=C)

    @pl.when(c == pl.num_programs(1) - 1)
    def _():
        for bi in range(BB):
            for h in range(HG_HEADS):
                sout_ref[bi, h] = ST[bi, h].T


def _hgrn_chunk(pq_ref, pf_ref, pi_ref, pg_ref, lb, on_ref, o_ref, ST, *, C):
    W2 = 2 * HG_DK
    NG = C // 8

    pq = pq_ref[...]
    pg = pg_ref[...]
    q = pq * _sigmoid(pq)
    f = lb + (1.0 - lb) * _sigmoid(pf_ref[...])
    k = 1.0 - f
    v = pi_ref[...]

    row = lax.broadcasted_iota(jnp.int32, (C, HG_HEADS * HG_DK), 0)
    b = jnp.log(f)
    sh = 1
    while sh < C:
        b = b + jnp.where(row >= sh, pltpu.roll(b, sh, axis=0), 0.0)
        sh *= 2
    blast = b[C - 1:C, :]
    qt = q * jnp.exp(b)
    kt = k * jnp.exp(blast - b)
    eblast = jnp.exp(blast)

    r2 = lax.broadcasted_iota(jnp.int32, (W2, W2), 0) // HG_DK
    c2 = lax.broadcasted_iota(jnp.int32, (W2, W2), 1) // HG_DK
    ones_blk = (r2 == c2).astype(BF16)
    b2 = b * LOG2E
    cs = b2 - jnp.log2(k)
    sub8 = lax.broadcasted_iota(jnp.int32, (8, HG_HEADS * HG_DK), 0)
    acc = [[jnp.zeros((8, W2), F32) for _ in range(NG)] for _ in range(HG_HEADS // 2)]
    for sg in range(NG):
        t0 = 8 * sg
        n = C - t0
        bt, qq = b2[t0:, :], q[t0:, :]
        pieces = []
        for s in range(t0, t0 + 8):
            d = bt - cs[s:s + 1, :]
            d = jnp.concatenate([jnp.where(sub8 >= s - t0, d[:8], -jnp.inf), d[8:]], axis=0) if n > 8 \
                else jnp.where(sub8 >= s - t0, d, -jnp.inf)
            pieces.append(jnp.exp2(d) * qq)
        E = jnp.concatenate(pieces, axis=0).astype(BF16)
        for hp in range(HG_HEADS // 2):
            lanes = slice(hp * W2, (hp + 1) * W2)
            R = jnp.dot(E[:, lanes], ones_blk, preferred_element_type=F32)
            for si in range(8):
                vrow = v[t0 + si:t0 + si + 1, lanes]
                for tg in range(n // 8):
                    blk = R[si * n + 8 * tg:si * n + 8 * tg + 8, :]
                    acc[hp][sg + tg] = acc[hp][sg + tg] + blk * vrow
    o_intra = jnp.concatenate([jnp.concatenate(a, axis=0) for a in acc], axis=1)

    onw = on_ref[...]
    qtb = _pad_rows(qt, 16).astype(BF16)
    ktb = _pad_rows(kt, 16).astype(BF16)
    vb = _pad_rows(v, 16).astype(BF16)
    for h in range(HG_HEADS):
        sl = slice(h * HG_DK, (h + 1) * HG_DK)
        Sh = ST[h]
        o_h = lax.dot_general(qtb[:, sl], Sh.astype(BF16), (((1,), (1,)), ((), ())),
                              preferred_element_type=F32)[:C] + o_intra[:, sl]
        on = o_h * lax.rsqrt(jnp.mean(o_h * o_h, axis=-1, keepdims=True) + EPS) * onw
        g = pg[:, sl]
        o_ref[:, sl] = on * (g * _sigmoid(g))
        upd = lax.dot_general(vb[:, sl], ktb[:, sl], (((0,), (0,)), ((), ())),
                              preferred_element_type=F32)
        ST[h] = Sh * eblast[:, sl] + upd


def _hgrn(proj, lb, onorm, state, chunk, bb, nch):
    B, T, _ = proj.shape
    width = HG_HEADS * HG_DK
    blk = chunk * nch
    col = lambda j: pl.BlockSpec((bb, blk, width), lambda b, c, j=j: (b, c, j))
    st = pl.BlockSpec((bb, HG_HEADS, HG_DK, HG_DV), lambda b, c: (b, 0, 0, 0))
    return pl.pallas_call(
        functools.partial(_hgrn_body, C=chunk, BB=bb, NCH=nch),
        grid=(B // bb, T // blk),
        in_specs=[col(COL_PQ // width), col(COL_PF // width), col(COL_PI // width), col(COL_PG // width),
                  pl.BlockSpec(lb.shape, lambda b, c: (0, 0)),
                  pl.BlockSpec((1, HG_DV), lambda b, c: (0, 0)),
                  st],
        out_specs=[pl.BlockSpec((bb, blk, width), lambda b, c: (b, c, 0)), st],
        out_shape=[jax.ShapeDtypeStruct((B, T, width), F32),
                   jax.ShapeDtypeStruct(state.shape, F32)],
        scratch_shapes=[pltpu.VMEM((bb, HG_HEADS, HG_DV, HG_DK), F32)],
        compiler_params=_params("parallel", "arbitrary"),
        name="hgrn",
    )(proj, proj, proj, proj, lb, onorm, state)


def _rope_tables(pos):
    half = HEAD_DIM // 2
    inv = ROPE_THETA ** (-jnp.arange(half, dtype=F32) / half)
    ang = pos.astype(F32)[:, None] * inv[None, :]
    cos, sin = jnp.cos(ang), jnp.sin(ang)
    cos = jnp.concatenate([cos, cos, cos, cos], axis=1)
    sin = jnp.concatenate([-sin, sin, -sin, sin], axis=1)
    return cos, sin


def _rope(x, cos, sin):
    lane = lax.broadcasted_iota(jnp.int32, (x.shape[0], LANES), 1)
    first = (lane % HEAD_DIM) < (HEAD_DIM // 2)
    outs = []
    for s in range(x.shape[1] // LANES):
        xs = x[:, s * LANES:(s + 1) * LANES]
        partner = jnp.where(first, pltpu.roll(xs, LANES - HEAD_DIM // 2, axis=1),
                            pltpu.roll(xs, HEAD_DIM // 2, axis=1))
        outs.append(xs * cos + partner * sin)
    return jnp.concatenate(outs, axis=1)


def _sink_softmax_pv(s, sink, vg):
    m = jnp.maximum(jnp.max(s, axis=-1, keepdims=True), sink)
    p = jnp.exp2(s - m)
    den = jnp.sum(p, axis=-1, keepdims=True) + jnp.exp2(sink - m)
    return jnp.dot(p.astype(BF16), vg, preferred_element_type=F32) / den


def _swa_prompt_body(q_ref, k_ref, v_ref, cos_ref, sin_ref, sink_ref, o_ref, krot_ref, kprev, vprev,
                     *, first_valid, NBK):
    i = pl.program_id(0)

    @pl.when(i == 0)
    def _():
        kprev[...] = jnp.zeros_like(kprev)
        vprev[...] = jnp.zeros_like(vprev)

    R = GROUP * WINDOW
    r = lax.broadcasted_iota(jnp.int32, (R, 2 * WINDOW), 0) % WINDOW
    j = lax.broadcasted_iota(jnp.int32, (R, 2 * WINDOW), 1)
    band = (j > r) & (j <= r + WINDOW)
    hrow = lax.broadcasted_iota(jnp.int32, (R, 1), 0) // WINDOW
    kp, vp = kprev[...], vprev[...]
    for bi in range(NBK):
        rows = slice(bi * WINDOW, (bi + 1) * WINDOW)
        cos, sin = cos_ref[rows, :], sin_ref[rows, :]
        qr = _rope(q_ref[rows, :], cos, sin)
        kr = _rope(k_ref[rows, :], cos, sin)
        v = v_ref[rows, :]
        krot_ref[rows, :] = kr
        kk = jnp.concatenate([kp, kr], axis=0).astype(BF16)
        vv = jnp.concatenate([vp, v], axis=0).astype(BF16)
        mask = band & ((i * NBK + bi - 1) * WINDOW + j >= first_valid)
        for g in range(ATT_KV_HEADS):
            kg = kk[:, g * HEAD_DIM:(g + 1) * HEAD_DIM]
            vg = vv[:, g * HEAD_DIM:(g + 1) * HEAD_DIM]
            qg = jnp.concatenate([qr[:, (g * GROUP + hh) * HEAD_DIM:(g * GROUP + hh + 1) * HEAD_DIM]
                                  for hh in range(GROUP)], axis=0).astype(BF16)
            sink = jnp.zeros((R, 1), F32)
            for hh in range(GROUP):
                sink = jnp.where(hrow == hh, sink_ref[g * GROUP + hh] * LOG2E, sink)
            s = lax.dot_general(qg, kg, (((1,), (1,)), ((), ())),
                                preferred_element_type=F32) * (HEAD_DIM ** -0.5 * LOG2E)
            o = _sink_softmax_pv(jnp.where(mask, s, -jnp.inf), sink, vg)
            for hh in range(GROUP):
                h = g * GROUP + hh
                o_ref[rows, h * HEAD_DIM:(h + 1) * HEAD_DIM] = o[hh * WINDOW:(hh + 1) * WINDOW]
        kp, vp = kr, v
    kprev[...] = kp
    vprev[...] = vp


def _swa_prompt(proj, cos, sin, sinks, first_valid, nbk):
    T = proj.shape[0]
    blk = nbk * WINDOW
    nb = T // blk
    return pl.pallas_call(
        functools.partial(_swa_prompt_body, first_valid=first_valid, NBK=nbk),
        grid=(nb,),
        in_specs=[pl.BlockSpec((blk, D_MODEL), lambda i: (i, COL_AQ // D_MODEL)),
                  pl.BlockSpec((blk, KV_WIDTH), lambda i: (i, COL_AK // KV_WIDTH)),
                  pl.BlockSpec((blk, KV_WIDTH), lambda i: (i, COL_AV // KV_WIDTH)),
                  pl.BlockSpec((blk, LANES), lambda i: (i, 0)),
                  pl.BlockSpec((blk, LANES), lambda i: (i, 0)),
                  pl.BlockSpec(memory_space=pltpu.SMEM)],
        out_specs=[pl.BlockSpec((blk, D_MODEL), lambda i: (i, 0)),
                   pl.BlockSpec((blk, KV_WIDTH), lambda i: (i, 0))],
        out_shape=[jax.ShapeDtypeStruct((T, D_MODEL), F32),
                   jax.ShapeDtypeStruct((T, KV_WIDTH), F32)],
        scratch_shapes=[pltpu.VMEM((WINDOW, KV_WIDTH), F32)] * 2,
        compiler_params=_params("arbitrary"),
        name="swa_prompt",
    )(proj, proj, proj, cos, sin, sinks)


def _swa_sample_body(q_ref, k_ref, v_ref, ck_ref, cv_ref, cos_ref, sin_ref, sink_ref, o_ref, ko_ref, vo_ref,
                     *, SB, L, WC):
    cos, sin = cos_ref[...], sin_ref[...]
    qr = _rope(q_ref[...], cos, sin)
    kr = _rope(k_ref[...], cos, sin)
    v = v_ref[...]
    for b in range(SB):
        ko_ref[b, 0:WC - L, :] = ck_ref[b, L:WC, :]
        ko_ref[b, WC - L:WC, :] = kr[b * L:(b + 1) * L, :]
        vo_ref[b, 0:WC - L, :] = cv_ref[b, L:WC, :]
        vo_ref[b, WC - L:WC, :] = v[b * L:(b + 1) * L, :]
    NB = SB * L
    R = GROUP * NB
    NC = SB * WC
    rr = lax.broadcasted_iota(jnp.int32, (R, NC + NB), 0)
    j = lax.broadcasted_iota(jnp.int32, (R, NC + NB), 1)
    qseq, qtok = (rr % NB) // L, rr % L
    cached = j < NC
    kseq = jnp.where(cached, j // WC, (j - NC) // L)
    rel = jnp.where(cached, qtok + WC - j % WC, qtok - (j - NC) % L)
    mask = (kseq == qseq) & (rel >= 0) & (rel < WINDOW)
    hrow = lax.broadcasted_iota(jnp.int32, (R, 1), 0) // NB
    for g in range(ATT_KV_HEADS):
        cols = slice(g * HEAD_DIM, (g + 1) * HEAD_DIM)
        keys = jnp.concatenate([ck_ref[b][:, cols] for b in range(SB)] + [kr[:, cols]], axis=0).astype(BF16)
        vals = jnp.concatenate([cv_ref[b][:, cols] for b in range(SB)] + [v[:, cols]], axis=0).astype(BF16)
        qg = jnp.concatenate([qr[:, (g * GROUP + hh) * HEAD_DIM:(g * GROUP + hh + 1) * HEAD_DIM]
                              for hh in range(GROUP)], axis=0).astype(BF16)
        sink = jnp.zeros((R, 1), F32)
        for hh in range(GROUP):
            sink = jnp.where(hrow == hh, sink_ref[g * GROUP + hh] * LOG2E, sink)
        s = lax.dot_general(qg, keys, (((1,), (1,)), ((), ())),
                            preferred_element_type=F32) * (HEAD_DIM ** -0.5 * LOG2E)
        o = _sink_softmax_pv(jnp.where(mask, s, -jnp.inf), sink, vals)
        for hh in range(GROUP):
            h = g * GROUP + hh
            o_ref[:, h * HEAD_DIM:(h + 1) * HEAD_DIM] = o[hh * NB:(hh + 1) * NB]


def _swa_sample(proj, cache_k, cache_v, cos, sin, sinks, L, SB):
    rows = proj.shape[0]
    Bd, WC, _ = cache_k.shape
    blk = SB * L
    return pl.pallas_call(
        functools.partial(_swa_sample_body, SB=SB, L=L, WC=WC),
        grid=(Bd // SB,),
        in_specs=[pl.BlockSpec((blk, D_MODEL), lambda i: (i, COL_AQ // D_MODEL)),
                  pl.BlockSpec((blk, KV_WIDTH), lambda i: (i, COL_AK // KV_WIDTH)),
                  pl.BlockSpec((blk, KV_WIDTH), lambda i: (i, COL_AV // KV_WIDTH)),
                  pl.BlockSpec((SB, WC, KV_WIDTH), lambda i: (i, 0, 0)),
                  pl.BlockSpec((SB, WC, KV_WIDTH), lambda i: (i, 0, 0)),
                  pl.BlockSpec((blk, LANES), lambda i: (0, 0)),
                  pl.BlockSpec((blk, LANES), lambda i: (0, 0)),
                  pl.BlockSpec(memory_space=pltpu.SMEM)],
        out_specs=[pl.BlockSpec((blk, D_MODEL), lambda i: (i, 0)),
                   pl.BlockSpec((SB, WC, KV_WIDTH), lambda i: (i, 0, 0)),
                   pl.BlockSpec((SB, WC, KV_WIDTH), lambda i: (i, 0, 0))],
        out_shape=[jax.ShapeDtypeStruct((rows, D_MODEL), F32),
                   jax.ShapeDtypeStruct(cache_k.shape, F32),
                   jax.ShapeDtypeStruct(cache_v.shape, F32)],
        compiler_params=_params("parallel"),
        name="swa_sample",
    )(proj, proj, proj, cache_k, cache_v, cos, sin, sinks)


def _merge_body(h_ref, oa_ref, ob_ref, ga_ref, gb_ref, w_ref, o_ref):
    m = _sigmoid(ga_ref[...]) * oa_ref[...] + _sigmoid(gb_ref[...]) * ob_ref[...]
    o_ref[...] = h_ref[...] + jnp.dot(m.astype(BF16), w_ref[...], preferred_element_type=F32)


def _merge(h, o_a, o_b, proj, w_out_b, tm):
    rows = h.shape[0]
    row = lambda j: pl.BlockSpec((tm, D_MODEL), lambda i, j=j: (i, j))
    return pl.pallas_call(
        _merge_body,
        grid=(rows // tm,),
        in_specs=[row(0), row(0), row(0), row(COL_GA // D_MODEL), row(COL_GB // D_MODEL),
                  pl.BlockSpec((D_MODEL, D_MODEL), lambda i: (0, 0))],
        out_specs=row(0),
        out_shape=jax.ShapeDtypeStruct((rows, D_MODEL), F32),
        compiler_params=_params("parallel"),
        name="merge",
    )(h, o_a, o_b, proj, proj, w_out_b)


def _top_values(x, n, ranked=False):
    vals = []
    rank = jnp.full(x.shape, float(n), F32)
    for it in range(n):
        m = jnp.max(x, axis=0, keepdims=True)
        vals.append(m)
        hit = x == m
        x = jnp.where(hit, -jnp.inf, x)
        if ranked:
            rank = jnp.where(hit, float(it), rank)
    top = jnp.concatenate(vals, axis=0)
    return (top, rank) if ranked else top


def _staircase_sums(t1, t2):
    K = PEER_TOPK
    sub = lax.broadcasted_iota(jnp.int32, (8, t1.shape[1]), 0)
    pieces = [t1[0:1, :] + t2, t1[1:2, :] + t2[0:8, :]]
    for i in range(2, 8):
        pieces.append(jnp.where(sub < K // (i + 1), t1[i:i + 1, :] + t2[0:8, :], -jnp.inf))
    pieces.append(t1[8:16, :] + t2[0:1, :])
    return jnp.concatenate(pieces, axis=0)


def _rows_bf16(row, rows):
    tile = jnp.broadcast_to(row, (16, row.shape[1])).astype(BF16)
    return jnp.concatenate([tile] * (rows // 16), axis=0)


RING = 3


def _peer_body(h_ref, nf_ref, nfin_ref, wq_ref, sub_ref, u0_ref, u_hbm, vt_hbm, o_ref,
               xnT, m1T, e1T, r2T, e2T, act_s, yT, ubuf, vbuf, sem, *, TB, EC, NB):
    i = pl.program_id(0)
    e = pl.program_id(1)
    ne = pl.num_programs(1)
    K = PEER_TOPK
    slot = e % 2
    SLABS = EC // (2 * PEER_NKEYS)
    step = i * ne + e

    def table_copies(e_of_step, ring_slot):
        u_chunk = (e_of_step + 1) % ne
        return (pltpu.make_async_copy(u_hbm.at[pl.ds(pl.multiple_of(u_chunk * EC, EC), EC), :],
                                      ubuf.at[ring_slot], sem.at[0, ring_slot]),
                pltpu.make_async_copy(vt_hbm.at[pl.ds(pl.multiple_of(e_of_step * SLABS, SLABS), SLABS)],
                                      vbuf.at[ring_slot], sem.at[1, ring_slot]))

    @pl.when(step == 0)
    def _():
        for d in range(RING - 1):
            for cp in table_copies(d, d):
                cp.start()

    @pl.when(step + RING - 1 < NB * ne)
    def _():
        for cp in table_copies((e + RING - 1) % ne, (step + RING - 1) % RING):
            cp.start()

    ring_slot = step % RING
    for cp in table_copies(e, ring_slot):
        cp.wait()
    un_ref = ubuf.at[ring_slot]
    vt_ref = vbuf.at[ring_slot]

    @pl.when(e == 0)
    def _():
        xt = _rms(h_ref[...], nf_ref[...]).T.astype(BF16)
        xnT[...] = xt
        yT[...] = jnp.zeros(yT.shape, F32)
        act_s[0] = jnp.dot(u0_ref[...], xt, preferred_element_type=F32)
        for h in range(PEER_HEADS):
            scores = []
            for p in range(2):
                hp = 2 * h + p
                qhp = jnp.dot(wq_ref[hp * LANES:(hp + 1) * LANES, :], xt, preferred_element_type=F32)
                scores.append(jnp.dot(sub_ref[hp], qhp.astype(BF16), preferred_element_type=F32))
            s1, s2 = scores
            t1 = _top_values(s1, K)
            t2, rank2 = _top_values(s2, K, ranked=True)
            c = _top_values(_staircase_sums(t1, t2), K)
            tau = c[K - 1:K, :]
            z = jnp.sum(jnp.exp(c - c[0:1, :]), axis=0, keepdims=True)
            m = jnp.zeros_like(s1)
            for j in range(4):
                m = jnp.where(s1 + t2[j:j + 1, :] >= tau, float(j + 1), m)
            for r in range(3):
                cnt = jnp.sum(jnp.where(t1[r:r + 1, :] + t2 >= tau, 1.0, 0.0), axis=0, keepdims=True)
                m = jnp.where(s1 == t1[r:r + 1, :], cnt, m)
            m1T[h] = m
            r2T[h] = rank2.astype(BF16)
            e1T[h] = jnp.exp(s1 - t1[0:1, :]) / z
            e2T[h] = jnp.exp(s2 - t2[0:1, :]).astype(BF16)

    act_next = jnp.dot(un_ref[...], xnT[...], preferred_element_type=F32)
    SLAB = 2 * PEER_NKEYS
    contrib = None
    for j in range(EC // SLAB):
        parts = []
        for al in range(2 * j, 2 * j + 2):
            a = e * (EC // PEER_NKEYS) + al
            w = jnp.zeros((PEER_NKEYS, TB), BF16)
            for h in range(PEER_HEADS):
                sel = r2T[h] < _rows_bf16(m1T[h, pl.ds(a, 1), :], PEER_NKEYS)
                gate = _rows_bf16(e1T[h, pl.ds(a, 1), :], PEER_NKEYS) * e2T[h]
                w = w + jnp.where(sel, gate, jnp.zeros_like(gate))
            x = act_s[slot, al * PEER_NKEYS:(al + 1) * PEER_NKEYS, :]
            gelu = 0.5 * x * (1.0 + lax.erf(x * (2.0 ** -0.5)))
            parts.append(w * gelu.astype(BF16))
        g = jnp.concatenate(parts, axis=0)
        d = jnp.dot(vt_ref[j], g, preferred_element_type=F32)
        contrib = d if contrib is None else contrib + d
    act_s[1 - slot] = act_next
    yT[...] += contrib

    @pl.when(e == pl.num_programs(1) - 1)
    def _():
        h3 = h_ref[...] + yT[...].T
        o_ref[...] = _rms(h3, nfin_ref[...])


def _peer(h2, row0, rows, nf, nfin, wqT_b, sub_b, u_b, vT_b, TB, EC):
    nexp = u_b.shape[0]
    ne = nexp // EC
    nb = rows // TB
    off = row0 // TB
    slab = 2 * PEER_NKEYS
    assert RING - 1 < ne and vT_b.shape == (nexp // slab, D_MODEL, slab)
    f = pl.pallas_call(
        functools.partial(_peer_body, TB=TB, EC=EC, NB=nb),
        grid=(nb, ne),
        in_specs=[pl.BlockSpec((TB, D_MODEL), lambda i, e: (i + off, 0)),
                  pl.BlockSpec((1, D_MODEL), lambda i, e: (0, 0)),
                  pl.BlockSpec((1, D_MODEL), lambda i, e: (0, 0)),
                  pl.BlockSpec(wqT_b.shape, lambda i, e: (0, 0)),
                  pl.BlockSpec(sub_b.shape, lambda i, e: (0, 0, 0)),
                  pl.BlockSpec((EC, D_MODEL), lambda i, e: (0, 0)),
                  pl.BlockSpec(memory_space=pl.ANY),
                  pl.BlockSpec(memory_space=pl.ANY)],
        out_specs=pl.BlockSpec((TB, D_MODEL), lambda i, e: (i, 0)),
        out_shape=jax.ShapeDtypeStruct((rows, D_MODEL), F32),
        scratch_shapes=[pltpu.VMEM((D_MODEL, TB), BF16),
                        pltpu.VMEM((PEER_HEADS, PEER_NKEYS, TB), F32),
                        pltpu.VMEM((PEER_HEADS, PEER_NKEYS, TB), F32),
                        pltpu.VMEM((PEER_HEADS, PEER_NKEYS, TB), BF16),
                        pltpu.VMEM((PEER_HEADS, PEER_NKEYS, TB), BF16),
                        pltpu.VMEM((2, EC, TB), F32),
                        pltpu.VMEM((D_MODEL, TB), F32),
                        pltpu.VMEM((RING, EC, D_MODEL), BF16),
                        pltpu.VMEM((RING, EC // slab, D_MODEL, slab), BF16),
                        pltpu.SemaphoreType.DMA((2, RING))],
        compiler_params=_params("arbitrary", "arbitrary"),
        name="peer",
    )
    return f(h2, nf, nfin, wqT_b, sub_b, u_b, u_b, vT_b)


PEER_TB = 256
PEER_EC = 1024
PROJ_TN = 1536
HGRN_CHUNK = 32
HGRN_NCH = 13
SWA_NBK = 1
SAMPLE_SB = 8
SAMPLE_HB = 8


def _row_tile(rows, cap):
    best = 8
    for t in range(8, cap + 1, 8):
        if rows % t == 0:
            best = t
    return best


def kernel(x_prompt, x_sample, state_hgrn, cache_swa_k, cache_swa_v, meta_tokens, w_in, hgrn_lb,
           hgrn_onorm, attn_sinks, w_out, norm_mix, norm_ffn, peer_query, peer_subkeys, peer_u, peer_v,
           norm_final):
    assert w_in.shape[0] == 1, "single-layer trunk"
    B, SEQ, _ = x_prompt.shape
    Bd, L, _ = x_sample.shape
    assert B == 1
    past = PAST_LEN
    wc = cache_swa_k.shape[2]
    START = PEER_TB
    T = START + SEQ
    first_valid = START - N_META

    sizes = np.cumsum([0, 1024, 1024, 1024, 1024, 1024, 256, 256, 1024, 1024])
    seg = lambda i: w_in[0][:, sizes[i]:sizes[i + 1]]
    w_in_b = jnp.concatenate([seg(0), seg(1), seg(2), seg(3), seg(4), seg(7), seg(8), seg(5), seg(6)],
                             axis=1).astype(BF16)
    w_out_b = w_out[0].astype(BF16)
    wqT_b = peer_query[0].T.astype(BF16)
    sub_b = peer_subkeys[0].reshape(PEER_HEADS * 2, PEER_NKEYS, -1).astype(BF16)
    u_b = peer_u[0].astype(BF16)
    vT_b = peer_v[0].astype(BF16).reshape(-1, 2 * PEER_NKEYS, D_MODEL).transpose(0, 2, 1)
    nmix = norm_mix[0].reshape(1, D_MODEL)
    nffn = norm_ffn[0].reshape(1, D_MODEL)
    nfin = norm_final.reshape(1, D_MODEL)
    onorm = hgrn_onorm[0].reshape(1, HG_DV)
    sinks = attn_sinks[0]

    hp = jnp.concatenate([jnp.zeros((first_valid, D_MODEL), F32), meta_tokens.astype(F32), x_prompt[0]],
                         axis=0)
    proj_p = _norm_proj(hp, nmix, w_in_b, _row_tile(T, 1280), PROJ_TN)
    oa_p, st_p = _hgrn(proj_p[None], hgrn_lb, onorm,
                       jnp.zeros((1, HG_HEADS, HG_DK, HG_DV), F32), HGRN_CHUNK, 1, HGRN_NCH)
    cos_p, sin_p = _rope_tables(jnp.arange(T) - first_valid)
    ob_p, krot_p = _swa_prompt(proj_p, cos_p, sin_p, sinks, first_valid, SWA_NBK)
    h2_p = _merge(hp, oa_p[0], ob_p, proj_p, w_out_b, _row_tile(T, 512))
    y_prompt = _peer(h2_p, START, SEQ, nffn, nfin, wqT_b, sub_b, u_b, vT_b, PEER_TB, PEER_EC)[None]
    wp = min(WINDOW, SEQ + N_META)
    kp = krot_p[T - wp:].reshape(1, 1, wp, ATT_KV_HEADS, HEAD_DIM)
    vp = proj_p[T - wp:, COL_AV:COL_AV + KV_WIDTH].reshape(1, 1, wp, ATT_KV_HEADS, HEAD_DIM)

    hs = x_sample.reshape(Bd * L, D_MODEL)
    proj_s = _norm_proj(hs, nmix, w_in_b, _row_tile(Bd * L, 512), PROJ_TN)
    oa_s, st_s = _hgrn(proj_s.reshape(Bd, L, PROJ_COLS), hgrn_lb, onorm, state_hgrn[0], L, SAMPLE_HB, 1)
    cos_s, sin_s = _rope_tables(past + jnp.arange(L))
    cos_s, sin_s = jnp.tile(cos_s, (SAMPLE_SB, 1)), jnp.tile(sin_s, (SAMPLE_SB, 1))
    ck = cache_swa_k[0].reshape(Bd, wc, KV_WIDTH)
    cv = cache_swa_v[0].reshape(Bd, wc, KV_WIDTH)
    ob_s, ks, vs = _swa_sample(proj_s, ck, cv, cos_s, sin_s, sinks, L, SAMPLE_SB)
    h2_s = _merge(hs, oa_s.reshape(Bd * L, D_MODEL), ob_s, proj_s, w_out_b, _row_tile(Bd * L, 512))
    y_sample = _peer(h2_s, 0, Bd * L, nffn, nfin, wqT_b, sub_b, u_b, vT_b, PEER_TB, PEER_EC)
    y_sample = y_sample.reshape(Bd, L, D_MODEL)
    ks = ks.reshape(1, Bd, wc, ATT_KV_HEADS, HEAD_DIM)
    vs = vs.reshape(1, Bd, wc, ATT_KV_HEADS, HEAD_DIM)

    return (y_prompt, y_sample, st_p[None], st_s[None], kp, ks, vp, vs)
```
